```python
import jax, jax.numpy as jnp
from jax import lax
import numpy as np

D_MODEL = 2048
BATCH = 16
SEQ = 2048
DEPTH = 4
DEC_BATCH = 16
DEC_SEQ = 16
PAST_LEN = 2048

CHUNK = 64
LEFT_CHUNKS = 8
ATT_WINDOW = LEFT_CHUNKS * CHUNK
D_MIX = D_MODEL
D_CONV = D_MIX // 2
N_HEADS = 8
HEAD_DIM = 128
D_ATTN = N_HEADS * HEAD_DIM
D_IN_PROJ = 3 * D_CONV + 3 * D_ATTN
CONV_WIDTH = 3
REL_CLIP = 2 * CHUNK
N_REL = 2 * REL_CLIP + 1
D_FF = ((4 * D_MODEL * 2 // 3) + 127) // 128 * 128
EPS = 1e-6
NEG_INF = -1e30

kernel_name = 'hymba_conformer_streaming_step'


def rmsnorm(x, g):
    xf = x.astype(jnp.float32)
    y = xf * lax.rsqrt(jnp.mean(xf * xf, axis=-1, keepdims=True) + EPS)
    return (y * g.astype(jnp.float32)).astype(x.dtype)


def swiglu(x, w_gate, w_up, w_down):
    return (jax.nn.silu(x @ w_gate) * (x @ w_up)) @ w_down


def rel_bias_lookup(table, dist):
    idx = jnp.clip(dist, -REL_CLIP, REL_CLIP) + REL_CLIP
    return jnp.transpose(table[idx], (2, 0, 1)).astype(jnp.float32)


def attend(q, k, v, bias, valid=None):
    s = jnp.einsum('bqhd,bkhd->bhqk', q, k).astype(jnp.float32) * (HEAD_DIM ** -0.5) + bias
    if valid is not None:
        s = jnp.where(valid, s, NEG_INF)
    p = jax.nn.softmax(s, axis=-1)
    return jnp.einsum('bhqk,bkhd->bqhd', p.astype(v.dtype), v)


def band_attention_prompt(q, k, v, table):
    b, s, h, d = q.shape
    n_chunks = s // CHUNK
    band = ATT_WINDOW + CHUNK
    pad = ((0, 0), (ATT_WINDOW, 0), (0, 0), (0, 0))
    kp = jnp.pad(k, pad)
    vp = jnp.pad(v, pad)
    r = jnp.arange(CHUNK)
    rb = jnp.arange(band)
    bias = rel_bias_lookup(table, r[:, None] + ATT_WINDOW - rb[None, :])

    def one_chunk(c):
        start = c * CHUNK
        qs = lax.dynamic_slice_in_dim(q, start, CHUNK, axis=1)
        ks = lax.dynamic_slice_in_dim(kp, start, band, axis=1)
        vs = lax.dynamic_slice_in_dim(vp, start, band, axis=1)
        valid = (start - ATT_WINDOW + rb) >= 0
        return attend(qs, ks, vs, bias, valid)

    out = lax.map(one_chunk, jnp.arange(n_chunks))
    return jnp.transpose(out, (1, 0, 2, 3, 4)).reshape(b, s, h, d)


def band_attention_sample(q, k_new, v_new, k_cache, v_cache, table):
    t = q.shape[1]
    n_cache = k_cache.shape[1]
    k_all = jnp.concatenate([k_cache, k_new], axis=1)
    v_all = jnp.concatenate([v_cache, v_new], axis=1)
    dist = jnp.arange(t)[:, None] + n_cache - jnp.arange(n_cache + t)[None, :]
    bias = rel_bias_lookup(table, dist)
    return attend(q, k_all, v_all, bias)


def layer(x, conv_state, k_cache, v_cache, p):
    b, s, _ = x.shape
    h = x + 0.5 * rmsnorm(swiglu(rmsnorm(x, p['ln_ffn1_pre']), p['ffn1_w_gate'], p['ffn1_w_up'], p['ffn1_w_down']), p['ln_ffn1_post'])
    u = rmsnorm(h, p['ln_mix_pre'])
    z = u @ p['w_in']
    bg, cg, xc, q, k, v = jnp.split(z, [D_CONV, 2 * D_CONV, 3 * D_CONV, 3 * D_CONV + D_ATTN, 3 * D_CONV + 2 * D_ATTN], axis=-1)
    gx = cg * xc
    if conv_state is None:
        gp = jnp.pad(gx, ((0, 0), (CONV_WIDTH - 1, 0), (0, 0)))
    else:
        gp = jnp.concatenate([conv_state.astype(gx.dtype), gx], axis=1)
    w = p['conv_w']
    conv = gp[:, 0:s] * w[0]
    for j in range(1, CONV_WIDTH):
        conv = conv + gp[:, j:j + s] * w[j]
    y_conv = bg * conv
    new_conv = gp[:, gp.shape[1] - (CONV_WIDTH - 1):]
    q = q.reshape(b, s, N_HEADS, HEAD_DIM)
    k = k.reshape(b, s, N_HEADS, HEAD_DIM)
    v = v.reshape(b, s, N_HEADS, HEAD_DIM)
    if k_cache is None:
        y_attn = band_attention_prompt(q, k, v, p['rel_bias'])
        keep = min(ATT_WINDOW, s)
        new_k = k[:, s - keep:]
        new_v = v[:, s - keep:]
    else:
        y_attn = band_attention_sample(q, k, v, k_cache.astype(k.dtype), v_cache.astype(v.dtype), p['rel_bias'])
        new_k = k
        new_v = v
    y_attn = y_attn.reshape(b, s, D_ATTN)
    mix = jnp.concatenate([rmsnorm(y_conv, p['g_conv_out']), rmsnorm(y_attn, p['g_attn_out'])], axis=-1)
    h = h + rmsnorm(mix @ p['w_out'], p['ln_mix_post'])
    y = h + 0.5 * rmsnorm(swiglu(rmsnorm(h, p['ln_ffn2_pre']), p['ffn2_w_gate'], p['ffn2_w_up'], p['ffn2_w_down']), p['ln_ffn2_post'])
    return y, new_conv, new_k, new_v


def setup_inputs(seed: int = 0) -> dict:
    key = jax.random.key(seed)
    ks = jax.random.split(key, 24)
    f32 = jnp.float32
    att_cache = min(ATT_WINDOW, PAST_LEN)

    def nrm(k, shape, scale):
        return jax.random.normal(k, shape, f32) * scale

    def gain(k, n):
        return 1.0 + 0.05 * jax.random.normal(k, (DEPTH, n), f32)

    return {
        'x_prompt': nrm(ks[0], (BATCH, SEQ, D_MODEL), 1.0),
        'x_sample': nrm(ks[1], (DEC_BATCH, DEC_SEQ, D_MODEL), 1.0),
        'cache_k': nrm(ks[2], (DEPTH, DEC_BATCH, att_cache, N_HEADS, HEAD_DIM), 1.0),
        'cache_v': nrm(ks[3], (DEPTH, DEC_BATCH, att_cache, N_HEADS, HEAD_DIM), 1.0),
        'state_conv': nrm(ks[4], (DEPTH, DEC_BATCH, CONV_WIDTH - 1, D_CONV), 1.0),
        'ln_ffn1_pre': gain(ks[5], D_MODEL),
        'ffn1_w_gate': nrm(ks[6], (DEPTH, D_MODEL, D_FF), D_MODEL ** -0.5),
        'ffn1_w_up': nrm(ks[7], (DEPTH, D_MODEL, D_FF), D_MODEL ** -0.5),
        'ffn1_w_down': nrm(ks[8], (DEPTH, D_FF, D_MODEL), D_FF ** -0.5),
        'ln_ffn1_post': gain(ks[9], D_MODEL),
        'ln_mix_pre': gain(ks[10], D_MODEL),
        'w_in': nrm(ks[11], (DEPTH, D_MODEL, D_IN_PROJ), D_MODEL ** -0.5),
        'conv_w': nrm(ks[12], (DEPTH, CONV_WIDTH, D_CONV), CONV_WIDTH ** -0.5),
        'rel_bias': nrm(ks[13], (DEPTH, N_REL, N_HEADS), 0.5),
        'g_conv_out': gain(ks[14], D_CONV),
        'g_attn_out': gain(ks[15], D_ATTN),
        'w_out': nrm(ks[16], (DEPTH, D_MIX, D_MODEL), D_MIX ** -0.5),
        'ln_mix_post': gain(ks[17], D_MODEL),
        'ln_ffn2_pre': gain(ks[18], D_MODEL),
        'ffn2_w_gate': nrm(ks[19], (DEPTH, D_MODEL, D_FF), D_MODEL ** -0.5),
        'ffn2_w_up': nrm(ks[20], (DEPTH, D_MODEL, D_FF), D_MODEL ** -0.5),
        'ffn2_w_down': nrm(ks[21], (DEPTH, D_FF, D_MODEL), D_FF ** -0.5),
        'ln_ffn2_post': gain(ks[22], D_MODEL),
    }


def reference(x_prompt, x_sample, cache_k, cache_v, state_conv,
              ln_ffn1_pre, ffn1_w_gate, ffn1_w_up, ffn1_w_down, ln_ffn1_post,
              ln_mix_pre, w_in, conv_w, rel_bias, g_conv_out, g_attn_out, w_out, ln_mix_post,
              ln_ffn2_pre, ffn2_w_gate, ffn2_w_up, ffn2_w_down, ln_ffn2_post):
    hp = x_prompt
    hs = x_sample
    kp_l, vp_l, cp_l, ks_l, vs_l, cs_l = [], [], [], [], [], []
    for l in range(DEPTH):
        p = dict(
            ln_ffn1_pre=ln_ffn1_pre[l], ffn1_w_gate=ffn1_w_gate[l], ffn1_w_up=ffn1_w_up[l],
            ffn1_w_down=ffn1_w_down[l], ln_ffn1_post=ln_ffn1_post[l],
            ln_mix_pre=ln_mix_pre[l], w_in=w_in[l], conv_w=conv_w[l], rel_bias=rel_bias[l],
            g_conv_out=g_conv_out[l], g_attn_out=g_attn_out[l], w_out=w_out[l], ln_mix_post=ln_mix_post[l],
            ln_ffn2_pre=ln_ffn2_pre[l], ffn2_w_gate=ffn2_w_gate[l], ffn2_w_up=ffn2_w_up[l],
            ffn2_w_down=ffn2_w_down[l], ln_ffn2_post=ln_ffn2_post[l])
        hp, c_p, k_p, v_p = layer(hp, None, None, None, p)
        hs, c_s, k_s, v_s = layer(hs, state_conv[l], cache_k[l], cache_v[l], p)
        kp_l.append(k_p); vp_l.append(v_p); cp_l.append(c_p)
        ks_l.append(k_s); vs_l.append(v_s); cs_l.append(c_s)
    k_prompt = jnp.stack(kp_l)
    v_prompt = jnp.stack(vp_l)
    conv_prompt = jnp.stack(cp_l)
    k_sample = jnp.stack(ks_l)
    v_sample = jnp.stack(vs_l)
    conv_sample = jnp.stack(cs_l)
    return (hp, hs, k_prompt, v_prompt, conv_prompt, k_sample, v_sample, conv_sample)
```

```python
import functools

import jax
import jax.numpy as jnp
from jax import lax
from jax.experimental import pallas as pl
from jax.experimental.pallas import tpu as pltpu

F32 = jnp.float32
BF16 = jnp.bfloat16

CHUNK = 64
LEFT_CHUNKS = 8
ATT_WINDOW = LEFT_CHUNKS * CHUNK
EPS = 1e-6
NEG_INF = -1e30

LANES = 128
MXU_DIM = 256
VMEM_LIMIT_BYTES = 56 * 1024 * 1024

PAIR = 2 * CHUNK
PAIR_KEYS = ATT_WINDOW + PAIR
BIAS_LANES = PAIR_KEYS + PAIR


def _rms(x, g):
    ms = jnp.mean(x * x, axis=-1, keepdims=True)
    return x * lax.rsqrt(ms + EPS) * g


def _params(semantics):
    return pltpu.CompilerParams(dimension_semantics=semantics, vmem_limit_bytes=VMEM_LIMIT_BYTES)


def _resident(shape):
    return pl.BlockSpec(shape, lambda *_: (0,) * len(shape), pipeline_mode=pl.Buffered(1))


def _ffn_kernel(x_ref, gpre_ref, wg_ref, wu_ref, wd_ref, gpost_ref, o_ref, u_ref, *, n_chunks):
    j = pl.program_id(1)

    @pl.when(j == 0)
    def _():
        u_ref[...] = _rms(x_ref[...], gpre_ref[...]).astype(BF16)
        o_ref[...] = jnp.zeros_like(o_ref)

    u = u_ref[...]
    g = jnp.dot(u, wg_ref[...], preferred_element_type=F32)
    up = jnp.dot(u, wu_ref[...], preferred_element_type=F32)
    h = (g * jax.nn.sigmoid(g) * up).astype(BF16)
    o_ref[...] += jnp.dot(h, wd_ref[...], preferred_element_type=F32)

    @pl.when(j == n_chunks - 1)
    def _():
        o_ref[...] = x_ref[...] + 0.5 * _rms(o_ref[...], gpost_ref[...])


def _ffn(x, g_pre, wg, wu, wd, g_post, *, tm, tf):
    t, d = x.shape
    f = wg.shape[1]
    assert t % tm == 0 and f % tf == 0
    n_chunks = f // tf
    return pl.pallas_call(
        functools.partial(_ffn_kernel, n_chunks=n_chunks),
        grid=(t // tm, n_chunks),
        in_specs=[
            pl.BlockSpec((tm, d), lambda i, j: (i, 0)),
            pl.BlockSpec((1, d), lambda i, j: (0, 0)),
            pl.BlockSpec((d, tf), lambda i, j: (0, j)),
            pl.BlockSpec((d, tf), lambda i, j: (0, j)),
            pl.BlockSpec((tf, d), lambda i, j: (j, 0)),
            pl.BlockSpec((1, d), lambda i, j: (0, 0)),
        ],
        out_specs=pl.BlockSpec((tm, d), lambda i, j: (i, 0)),
        out_shape=jax.ShapeDtypeStruct((t, d), F32),
        scratch_shapes=[pltpu.VMEM((tm, d), BF16)],
        compiler_params=_params(("arbitrary", "arbitrary")),
        name="ffn",
    )(x, g_pre, wg, wu, wd, g_post)


def _proj_kernel(h_ref, gpre_ref, win_ref, convw_ref, cinit_ref, gc_ref,
                 q_ref, k_ref, v_ref, mixc_ref, cst_ref, kt_ref, vt_ref, carry_ref,
                 *, tm, seq_len, d_conv, d_attn, tail_tiles):
    i = pl.program_id(0)
    u = _rms(h_ref[...], gpre_ref[...]).astype(BF16)

    def proj(c0, n):
        return jnp.dot(u, win_ref[:, c0:c0 + n], preferred_element_type=F32)

    bg = proj(0, d_conv)
    gx = proj(d_conv, d_conv) * proj(2 * d_conv, d_conv)

    g1 = pltpu.roll(gx, 1, axis=0)
    g2 = pltpu.roll(gx, 2, axis=0)
    row = lax.broadcasted_iota(jnp.int32, (tm, 1), 0)
    if seq_len >= tm:
        tiles_per_seq = seq_len // tm
        ti = i % tiles_per_seq
        first = ti == 0
        halo0 = jnp.where(first, cinit_ref[0, 0:1, :], carry_ref[0:1, :])
        halo1 = jnp.where(first, cinit_ref[0, 1:2, :], carry_ref[1:2, :])
        pos = row
        carry_ref[...] = gx[tm - 2:tm, :]
        cst_ref[0] = gx[tm - 2:tm, :]
    else:
        n_seq = tm // seq_len
        halo0 = jnp.broadcast_to(cinit_ref[:, 0:1, :], (n_seq, seq_len, d_conv)).reshape(tm, d_conv)
        halo1 = jnp.broadcast_to(cinit_ref[:, 1:2, :], (n_seq, seq_len, d_conv)).reshape(tm, d_conv)
        pos = row % seq_len
        ti = tail_tiles - 1
        cst_ref[...] = gx.reshape(n_seq, seq_len, d_conv)[:, seq_len - 2:seq_len, :]
    g1 = jnp.where(pos == 0, halo1, g1)
    g2 = jnp.where(pos == 0, halo0, jnp.where(pos == 1, halo1, g2))
    w = convw_ref[...]
    y = bg * (g2 * w[0:1, :] + g1 * w[1:2, :] + gx * w[2:3, :])
    mixc_ref[...] = _rms(y, gc_ref[...]).astype(BF16)

    q_ref[...] = proj(3 * d_conv, d_attn).astype(BF16)
    k = proj(3 * d_conv + d_attn, d_attn)
    v = proj(3 * d_conv + 2 * d_attn, d_attn)
    k_ref[...] = k.astype(BF16)
    v_ref[...] = v.astype(BF16)

    if seq_len >= tm:
        @pl.when(ti >= seq_len // tm - tail_tiles)
        def _():
            kt_ref[...] = k
            vt_ref[...] = v
    else:
        kt_ref[...] = k
        vt_ref[...] = v


def _proj(h, g_pre, w_in, conv_w, conv_init, g_conv, *, tm, seq_len, d_conv, d_attn):
    t, d = h.shape
    n_seq = t // seq_len
    keep = min(ATT_WINDOW, seq_len)
    if seq_len >= tm:
        assert seq_len % tm == 0 and keep % tm == 0
        tiles_per_seq = seq_len // tm
        tail_tiles = keep // tm
        seq_of = lambda i: i // tiles_per_seq
        cinit_spec = pl.BlockSpec((1, 2, d_conv), lambda i: (seq_of(i), 0, 0))
        tail_map = lambda i: (seq_of(i) * tail_tiles
                              + jnp.maximum(i % tiles_per_seq - (tiles_per_seq - tail_tiles), 0), 0)
    else:
        assert tm % seq_len == 0 and keep == seq_len
        tail_tiles = 1
        cinit_spec = pl.BlockSpec((tm // seq_len, 2, d_conv), lambda i: (i, 0, 0))
        tail_map = lambda i: (i, 0)
    tile = lambda n: pl.BlockSpec((tm, n), lambda i: (i, 0))
    kern = functools.partial(_proj_kernel, tm=tm, seq_len=seq_len, d_conv=d_conv, d_attn=d_attn,
                             tail_tiles=tail_tiles)
    return pl.pallas_call(
        kern,
        grid=(t // tm,),
        in_specs=[
            tile(d),
            _resident((1, d)),
            _resident(w_in.shape),
            _resident(conv_w.shape),
            cinit_spec,
            _resident((1, d_conv)),
        ],
        out_specs=[
            tile(d_attn), tile(d_attn), tile(d_attn), tile(d_conv),
            cinit_spec,
            pl.BlockSpec((tm, d_attn), tail_map),
            pl.BlockSpec((tm, d_attn), tail_map),
        ],
        out_shape=[
            jax.ShapeDtypeStruct((t, d_attn), BF16),
            jax.ShapeDtypeStruct((t, d_attn), BF16),
            jax.ShapeDtypeStruct((t, d_attn), BF16),
            jax.ShapeDtypeStruct((t, d_conv), BF16),
            jax.ShapeDtypeStruct((n_seq, 2, d_conv), F32),
            jax.ShapeDtypeStruct((n_seq * keep, d_attn), F32),
            jax.ShapeDtypeStruct((n_seq * keep, d_attn), F32),
        ],
        scratch_shapes=[pltpu.VMEM((2, d_conv), F32)],
        compiler_params=_params(("arbitrary",)),
        name="proj",
    )(h, g_pre, w_in, conv_w, conv_init, g_conv)


def _toeplitz_bias(tvec_ref, h, rows):
    t = jnp.broadcast_to(tvec_ref[h:h + 1, :], (rows, BIAS_LANES))
    return pltpu.roll(t, 0, axis=1, stride=1, stride_axis=0)


def _attn_prompt_kernel(q_ref, kp_ref, kc_ref, vp_ref, vc_ref, tvec_ref, ga_ref, o_ref,
                        kbuf, vbuf, bias_ref, y_ref, *, n_heads, head_dim, blk):
    qb = pl.program_id(1)

    @pl.when((pl.program_id(0) == 0) & (qb == 0))
    def _():
        qi = lax.broadcasted_iota(jnp.int32, (PAIR, PAIR_KEYS), 0)
        kk = lax.broadcasted_iota(jnp.int32, (PAIR, PAIR_KEYS), 1)
        lo = (qi // CHUNK) * CHUNK
        band = (kk >= lo) & (kk < lo + ATT_WINDOW + CHUNK)
        for h in range(n_heads):
            bias_ref[h] = jnp.where(band, _toeplitz_bias(tvec_ref, h, PAIR)[:, :PAIR_KEYS], NEG_INF)

    kbuf[0:blk, :] = kp_ref[...]
    kbuf[blk:2 * blk, :] = kc_ref[...]
    vbuf[0:blk, :] = vp_ref[...]
    vbuf[blk:2 * blk, :] = vc_ref[...]

    scale = head_dim ** -0.5
    key_idx = lax.broadcasted_iota(jnp.int32, (1, PAIR_KEYS), 1)

    def pair_body(p, carry):
        r0 = pl.multiple_of(p * PAIR, PAIR)
        missing = (qb == 0) & (key_idx < blk - r0)
        keymask = jnp.where(missing, NEG_INF, 0.0)
        for h in range(n_heads):
            c0 = h * head_dim
            q = q_ref[pl.ds(r0, PAIR), c0:c0 + head_dim]
            k = kbuf[pl.ds(r0, PAIR_KEYS), c0:c0 + head_dim]
            v = vbuf[pl.ds(r0, PAIR_KEYS), c0:c0 + head_dim]
            s = lax.dot_general(q, k, (((1,), (1,)), ((), ())), preferred_element_type=F32)
            s = s * scale + bias_ref[h] + keymask
            e = jnp.exp(s - jnp.max(s, axis=-1, keepdims=True))
            l = jnp.sum(e, axis=-1, keepdims=True)
            o = jnp.dot(e.astype(BF16), v, preferred_element_type=F32)
            y_ref[pl.ds(r0, PAIR), c0:c0 + head_dim] = o / l
        return carry

    lax.fori_loop(0, blk // PAIR, pair_body, 0)
    o_ref[...] = _rms(y_ref[...], ga_ref[...]).astype(BF16)


def _attn_prompt(q, k, v, tvec, g_attn, *, n_seq, seq_len, n_heads, head_dim):
    t, d_attn = q.shape
    blk = ATT_WINDOW
    assert seq_len % blk == 0 and blk % PAIR == 0
    nb = seq_len // blk
    cur = pl.BlockSpec((blk, d_attn), lambda b, i: (b * nb + i, 0))
    prev = pl.BlockSpec((blk, d_attn), lambda b, i: (b * nb + jnp.maximum(i - 1, 0), 0))
    kern = functools.partial(_attn_prompt_kernel, n_heads=n_heads, head_dim=head_dim, blk=blk)
    return pl.pallas_call(
        kern,
        grid=(n_seq, nb),
        in_specs=[cur, prev, cur, prev, cur, _resident(tvec.shape), _resident((1, d_attn))],
        out_specs=cur,
        out_shape=jax.ShapeDtypeStruct((t, d_attn), BF16),
        scratch_shapes=[
            pltpu.VMEM((2 * blk, d_attn), BF16),
            pltpu.VMEM((2 * blk, d_attn), BF16),
            pltpu.VMEM((n_heads, PAIR, PAIR_KEYS), F32),
            pltpu.VMEM((blk, d_attn), F32),
        ],
        compiler_params=_params(("arbitrary", "arbitrary")),
        name="attn_prompt",
    )(q, k, k, v, v, tvec, g_attn)


def _attn_sample_kernel(q_ref, kn_ref, vn_ref, kc_ref, vc_ref, tvec_ref, ga_ref, o_ref, y_ref,
                        *, n_heads, head_dim, n_new, n_cache):
    scale = head_dim ** -0.5
    for h in range(n_heads):
        c0 = h * head_dim
        bias = _toeplitz_bias(tvec_ref, h, n_new)
        q = q_ref[:, c0:c0 + head_dim]
        kc = kc_ref[0, :, c0:c0 + head_dim].astype(BF16)
        vc = vc_ref[0, :, c0:c0 + head_dim].astype(BF16)
        kn = kn_ref[:, c0:c0 + head_dim]
        vn = vn_ref[:, c0:c0 + head_dim]
        nt = (((1,), (1,)), ((), ()))
        sc = lax.dot_general(q, kc, nt, preferred_element_type=F32) * scale + bias[:, :n_cache]
        sn = (lax.dot_general(q, kn, nt, preferred_element_type=F32) * scale
              + bias[:, n_cache:n_cache + n_new])
        m = jnp.maximum(jnp.max(sc, axis=-1, keepdims=True), jnp.max(sn, axis=-1, keepdims=True))
        ec = jnp.exp(sc - m)
        en = jnp.exp(sn - m)
        l = jnp.sum(ec, axis=-1, keepdims=True) + jnp.sum(en, axis=-1, keepdims=True)
        o = (jnp.dot(ec.astype(BF16), vc, preferred_element_type=F32)
             + jnp.dot(en.astype(BF16), vn, preferred_element_type=F32))
        y_ref[:, c0:c0 + head_dim] = o / l
    o_ref[...] = _rms(y_ref[...], ga_ref[...]).astype(BF16)


def _attn_sample(q, k, v, cache_k, cache_v, tvec, g_attn, *, n_seq, n_new, n_heads, head_dim):
    t, d_attn = q.shape
    n_cache = cache_k.shape[1]
    assert n_cache == ATT_WINDOW and n_new <= PAIR
    new = pl.BlockSpec((n_new, d_attn), lambda b: (b, 0))
    cache = pl.BlockSpec((1, n_cache, d_attn), lambda b: (b, 0, 0))
    kern = functools.partial(_attn_sample_kernel, n_heads=n_heads, head_dim=head_dim,
                             n_new=n_new, n_cache=n_cache)
    return pl.pallas_call(
        kern,
        grid=(n_seq,),
        in_specs=[new, new, new, cache, cache, _resident(tvec.shape), _resident((1, d_attn))],
        out_specs=new,
        out_shape=jax.ShapeDtypeStruct((t, d_attn), BF16),
        scratch_shapes=[pltpu.VMEM((n_new, d_attn), F32)],
        compiler_params=_params(("arbitrary",)),
        name="attn_sample",
    )(q, k, v, cache_k, cache_v, tvec, g_attn)


def _mix_out_kernel(mc_ref, ma_ref, wo_ref, gpost_ref, h_ref, o_ref, *, d_conv):
    z = (jnp.dot(mc_ref[...], wo_ref[0:d_conv, :], preferred_element_type=F32)
         + jnp.dot(ma_ref[...], wo_ref[d_conv:, :], preferred_element_type=F32))
    o_ref[...] = h_ref[...] + _rms(z, gpost_ref[...])


def _mix_out(mix_c, mix_a, w_out, g_post, h, *, tm):
    t, d = h.shape
    d_conv = mix_c.shape[1]
    d_attn = mix_a.shape[1]
    tile = lambda n: pl.BlockSpec((tm, n), lambda i: (i, 0))
    return pl.pallas_call(
        functools.partial(_mix_out_kernel, d_conv=d_conv),
        grid=(t // tm,),
        in_specs=[tile(d_conv), tile(d_attn), _resident(w_out.shape), _resident((1, d)), tile(d)],
        out_specs=tile(d),
        out_shape=jax.ShapeDtypeStruct((t, d), F32),
        compiler_params=_params(("arbitrary",)),
        name="mix_out",
    )(mix_c, mix_a, w_out, g_post, h)


def _pad_axis(w, axis, mult):
    pad = [(0, 0)] * w.ndim
    pad[axis] = (0, -w.shape[axis] % mult)
    return jnp.pad(w, pad)


def _bias_vector(table):
    clip = (table.shape[0] - 1) // 2
    far = table[2 * clip]
    n_far = ATT_WINDOW - clip + 1
    ramp = table[2 * clip - 1:0:-1]
    tail = BIAS_LANES - n_far - ramp.shape[0]
    t = jnp.concatenate([jnp.broadcast_to(far, (n_far,) + far.shape), ramp,
                         jnp.broadcast_to(far, (tail,) + far.shape)], axis=0)
    return t.T.astype(F32)


def kernel(x_prompt, x_sample, cache_k, cache_v, state_conv, ln_ffn1_pre, ffn1_w_gate, ffn1_w_up,
           ffn1_w_down, ln_ffn1_post, ln_mix_pre, w_in, conv_w, rel_bias, g_conv_out, g_attn_out, w_out,
           ln_mix_post, ln_ffn2_pre, ffn2_w_gate, ffn2_w_up, ffn2_w_down, ln_ffn2_post):
    batch, seq, d = x_prompt.shape
    dec_batch, dec_seq, _ = x_sample.shape
    depth = w_in.shape[0]
    n_heads, head_dim = cache_k.shape[3], cache_k.shape[4]
    d_attn = n_heads * head_dim
    d_conv = conv_w.shape[2]
    n_cache = cache_k.shape[2]
    keep = min(ATT_WINDOW, seq)

    tf = 2 * MXU_DIM
    tm_p, tm_s = 512, dec_batch * dec_seq
    tm_proj = 256

    hp = x_prompt.reshape(batch * seq, d)
    hs = x_sample.reshape(dec_batch * dec_seq, d)
    row = lambda a: a.reshape(1, -1)
    zero_state = jnp.zeros((batch, 2, d_conv), F32)

    outs = {n: [] for n in ("kp", "vp", "cp", "ks", "vs", "cs")}
    for l in range(depth):
        ffn_w = []
        for wg, wu, wd in ((ffn1_w_gate, ffn1_w_up, ffn1_w_down), (ffn2_w_gate, ffn2_w_up, ffn2_w_down)):
            ffn_w.append((_pad_axis(wg[l].astype(BF16), 1, tf), _pad_axis(wu[l].astype(BF16), 1, tf),
                          _pad_axis(wd[l].astype(BF16), 0, tf)))
        w_in_l = w_in[l].astype(BF16)
        w_out_l = w_out[l].astype(BF16)
        tvec = _bias_vector(rel_bias[l])

        def layer(h, conv_init, tm, tm_pj, seq_len, attend):
            n_seq = h.shape[0] // seq_len
            h = _ffn(h, row(ln_ffn1_pre[l]), *ffn_w[0], row(ln_ffn1_post[l]), tm=tm, tf=tf)
            q, k, v, mix_c, conv_new, k_keep, v_keep = _proj(
                h, row(ln_mix_pre[l]), w_in_l, conv_w[l], conv_init, row(g_conv_out[l]),
                tm=tm_pj, seq_len=seq_len, d_conv=d_conv, d_attn=d_attn)
            mix_a = attend(q, k, v, n_seq)
            h = _mix_out(mix_c, mix_a, w_out_l, row(ln_mix_post[l]), h, tm=tm)
            h = _ffn(h, row(ln_ffn2_pre[l]), *ffn_w[1], row(ln_ffn2_post[l]), tm=tm, tf=tf)
            return h, conv_new, k_keep, v_keep

        def attend_prompt(q, k, v, n_seq):
            return _attn_prompt(q, k, v, tvec, row(g_attn_out[l]), n_seq=n_seq, seq_len=seq,
                                n_heads=n_heads, head_dim=head_dim)

        def attend_sample(q, k, v, n_seq):
            return _attn_sample(q, k, v, cache_k[l].reshape(dec_batch, n_cache, d_attn),
                                cache_v[l].reshape(dec_batch, n_cache, d_attn), tvec, row(g_attn_out[l]),
                                n_seq=n_seq, n_new=dec_seq, n_heads=n_heads, head_dim=head_dim)

        hp, c_p, k_p, v_p = layer(hp, zero_state, tm_p, tm_proj, seq, attend_prompt)
        hs, c_s, k_s, v_s = layer(hs, state_conv[l], tm_s, tm_s, dec_seq, attend_sample)
        outs["kp"].append(k_p.reshape(batch, keep, n_heads, head_dim))
        outs["vp"].append(v_p.reshape(batch, keep, n_heads, head_dim))
        outs["cp"].append(c_p)
        outs["ks"].append(k_s.reshape(dec_batch, dec_seq, n_heads, head_dim))
        outs["vs"].append(v_s.reshape(dec_batch, dec_seq, n_heads, head_dim))
        outs["cs"].append(c_s)

    return (hp.reshape(batch, seq, d), hs.reshape(dec_batch, dec_seq, d),
            jnp.stack(outs["kp"]), jnp.stack(outs["vp"]), jnp.stack(outs["cp"]),
            jnp.stack(outs["ks"]), jnp.stack(outs["vs"]), jnp.stack(outs["cs"]))
```

```python
import functools

import jax
import jax.numpy as jnp
from jax import lax
from jax.experimental import pallas as pl
from jax.experimental.pallas import tpu as pltpu

F32 = jnp.float32
BF16 = jnp.bfloat16

CHUNK = 64
LEFT_CHUNKS = 8
ATT_WINDOW = LEFT_CHUNKS * CHUNK
EPS = 1e-6
NEG_INF = -1e30

MXU_DIM = 256
VMEM_LIMIT_BYTES = 60 * 1024 * 1024

PAIR = 2 * CHUNK
PAIR_KEYS = ATT_WINDOW + PAIR
BIAS_LANES = PAIR_KEYS + PAIR


def _rms(x, g):
    ms = jnp.mean(x * x, axis=-1, keepdims=True)
    return x * lax.rsqrt(ms + EPS) * g


def _params(semantics):
    return pltpu.CompilerParams(dimension_semantics=semantics, vmem_limit_bytes=VMEM_LIMIT_BYTES)


def _resident(layer, shape):
    return pl.BlockSpec((None,) + tuple(shape), lambda *_: (layer,) + (0,) * len(shape),
                        pipeline_mode=pl.Buffered(1))


def _ffn_kernel(x_ref, gpre_ref, wg_ref, wu_ref, wd_ref, gpost_ref, o_ref, u_ref, *, n_chunks, rc):
    j = pl.program_id(1)
    tm = x_ref.shape[0]

    def gate_up(u):
        g = jnp.dot(u, wg_ref[...], preferred_element_type=F32)
        up = jnp.dot(u, wu_ref[...], preferred_element_type=F32)
        return (g * jax.nn.sigmoid(g) * up).astype(BF16)

    def down(h):
        return jnp.dot(h, wd_ref[...], preferred_element_type=F32)

    @pl.when(j == 0)
    def _():
        for r in range(0, tm, rc):
            u = _rms(x_ref[r:r + rc, :], gpre_ref[...]).astype(BF16)
            u_ref[r:r + rc, :] = u
            o_ref[r:r + rc, :] = down(gate_up(u))

    @pl.when((j > 0) & (j < n_chunks - 1))
    def _():
        o_ref[...] += down(gate_up(u_ref[...]))

    @pl.when(j == n_chunks - 1)
    def _():
        h = gate_up(u_ref[...])
        for r in range(0, tm, rc):
            acc = o_ref[r:r + rc, :] + down(h[r:r + rc, :])
            o_ref[r:r + rc, :] = x_ref[r:r + rc, :] + 0.5 * _rms(acc, gpost_ref[...])


def _ffn(x, g_pre, wg, wu, wd, g_post, *, layer, tm, tf):
    t, d = x.shape
    f = wg.shape[2]
    assert t % tm == 0 and f % tf == 0 and f // tf >= 2
    n_chunks = f // tf
    rc = min(tm, MXU_DIM)
    gain = pl.BlockSpec((None, 1, d), lambda i, j: (layer, 0, 0))
    return pl.pallas_call(
        functools.partial(_ffn_kernel, n_chunks=n_chunks, rc=rc),
        grid=(t // tm, n_chunks),
        in_specs=[
            pl.BlockSpec((tm, d), lambda i, j: (i, 0)),
            gain,
            pl.BlockSpec((None, d, tf), lambda i, j: (layer, 0, j)),
            pl.BlockSpec((None, d, tf), lambda i, j: (layer, 0, j)),
            pl.BlockSpec((None, tf, d), lambda i, j: (layer, j, 0)),
            gain,
        ],
        out_specs=pl.BlockSpec((tm, d), lambda i, j: (i, 0)),
        out_shape=jax.ShapeDtypeStruct((t, d), F32),
        scratch_shapes=[pltpu.VMEM((tm, d), BF16)],
        compiler_params=_params(("arbitrary", "arbitrary")),
        name="ffn",
    )(x, g_pre, wg, wu, wd, g_post)


def _proj_kernel(h_ref, gpre_ref, win_ref, convw_ref, cinit_ref, gc_ref,
                 q_ref, k_ref, v_ref, mixc_ref, cst_ref, kt_ref, vt_ref, carry_ref,
                 *, tm, seq_len, d_conv, d_attn, tail_tiles):
    u = _rms(h_ref[...], gpre_ref[...]).astype(BF16)

    def proj(c0, n):
        return jnp.dot(u, win_ref[:, c0:c0 + n], preferred_element_type=F32)

    bg = proj(0, d_conv)
    gx = proj(d_conv, d_conv) * proj(2 * d_conv, d_conv)

    g1 = pltpu.roll(gx, 1, axis=0)
    g2 = pltpu.roll(gx, 2, axis=0)
    row = lax.broadcasted_iota(jnp.int32, (tm, 1), 0)
    if seq_len >= tm:
        ti = pl.program_id(0) % (seq_len // tm)
        first = ti == 0
        halo0 = jnp.where(first, cinit_ref[0, 0:1, :], carry_ref[0:1, :])
        halo1 = jnp.where(first, cinit_ref[0, 1:2, :], carry_ref[1:2, :])
        pos = row
        carry_ref[...] = gx[tm - 2:tm, :]
        cst_ref[0] = gx[tm - 2:tm, :]
    else:
        n_seq = tm // seq_len
        halo0 = jnp.broadcast_to(cinit_ref[:, 0:1, :], (n_seq, seq_len, d_conv)).reshape(tm, d_conv)
        halo1 = jnp.broadcast_to(cinit_ref[:, 1:2, :], (n_seq, seq_len, d_conv)).reshape(tm, d_conv)
        pos = row % seq_len
        cst_ref[...] = gx.reshape(n_seq, seq_len, d_conv)[:, seq_len - 2:seq_len, :]
    g1 = jnp.where(pos == 0, halo1, g1)
    g2 = jnp.where(pos == 0, halo0, jnp.where(pos == 1, halo1, g2))
    w = convw_ref[...]
    y = bg * (g2 * w[0:1, :] + g1 * w[1:2, :] + gx * w[2:3, :])
    mixc_ref[...] = _rms(y, gc_ref[...]).astype(BF16)

    q_ref[...] = proj(3 * d_conv, d_attn).astype(BF16)
    k = proj(3 * d_conv + d_attn, d_attn)
    v = proj(3 * d_conv + 2 * d_attn, d_attn)
    k_ref[...] = k.astype(BF16)
    v_ref[...] = v.astype(BF16)

    if seq_len >= tm:
        @pl.when(ti >= seq_len // tm - tail_tiles)
        def _():
            kt_ref[...] = k
            vt_ref[...] = v
    else:
        kt_ref[...] = k
        vt_ref[...] = v


def _proj(h, g_pre, w_in, conv_w, conv_init, g_conv, *, layer, init_layer, tm, seq_len, d_attn):
    t, d = h.shape
    d_conv = conv_w.shape[2]
    n_seq = t // seq_len
    keep = min(ATT_WINDOW, seq_len)
    if seq_len >= tm:
        assert seq_len % tm == 0 and keep % tm == 0
        tiles_per_seq = seq_len // tm
        tail_tiles = keep // tm
        seq_of = lambda i: i // tiles_per_seq
        state_block = (1, 2, d_conv)
        state_map = lambda i: (seq_of(i), 0, 0)
        tail_map = lambda i: (seq_of(i) * tail_tiles
                              + jnp.maximum(i % tiles_per_seq - (tiles_per_seq - tail_tiles), 0), 0)
    else:
        assert tm % seq_len == 0 and keep == seq_len
        tail_tiles = 1
        state_block = (tm // seq_len, 2, d_conv)
        state_map = lambda i: (i, 0, 0)
        tail_map = lambda i: (i, 0)
    tile = lambda n: pl.BlockSpec((tm, n), lambda i: (i, 0))
    kern = functools.partial(_proj_kernel, tm=tm, seq_len=seq_len, d_conv=d_conv, d_attn=d_attn,
                             tail_tiles=tail_tiles)
    return pl.pallas_call(
        kern,
        grid=(t // tm,),
        in_specs=[
            tile(d),
            _resident(layer, (1, d)),
            _resident(layer, w_in.shape[1:]),
            _resident(layer, conv_w.shape[1:]),
            pl.BlockSpec((None,) + state_block, lambda i: (init_layer,) + state_map(i)),
            _resident(layer, (1, d_conv)),
        ],
        out_specs=[
            tile(d_attn), tile(d_attn), tile(d_attn), tile(d_conv),
            pl.BlockSpec(state_block, state_map),
            pl.BlockSpec((tm, d_attn), tail_map),
            pl.BlockSpec((tm, d_attn), tail_map),
        ],
        out_shape=[
            jax.ShapeDtypeStruct((t, d_attn), BF16),
            jax.ShapeDtypeStruct((t, d_attn), BF16),
            jax.ShapeDtypeStruct((t, d_attn), BF16),
            jax.ShapeDtypeStruct((t, d_conv), BF16),
            jax.ShapeDtypeStruct((n_seq, 2, d_conv), F32),
            jax.ShapeDtypeStruct((n_seq * keep, d_attn), F32),
            jax.ShapeDtypeStruct((n_seq * keep, d_attn), F32),
        ],
        scratch_shapes=[pltpu.VMEM((2, d_conv), F32)],
        compiler_params=_params(("arbitrary",)),
        name="proj",
    )(h, g_pre, w_in, conv_w, conv_init, g_conv)


def _toeplitz_bias(tvec_ref, h, rows):
    t = jnp.broadcast_to(tvec_ref[h:h + 1, :], (rows, BIAS_LANES))
    return pltpu.roll(t, 0, axis=1, stride=1, stride_axis=0)


def _attn_prompt_kernel(q_ref, kp_ref, kc_ref, vp_ref, vc_ref, tvec_ref, ga_ref, o_ref,
                        kbuf, vbuf, bias_ref, y_ref, *, n_heads, head_dim, blk):
    qb = pl.program_id(1)

    @pl.when((pl.program_id(0) == 0) & (qb == 0))
    def _():
        qi = lax.broadcasted_iota(jnp.int32, (PAIR, PAIR_KEYS), 0)
        kk = lax.broadcasted_iota(jnp.int32, (PAIR, PAIR_KEYS), 1)
        lo = (qi // CHUNK) * CHUNK
        band = (kk >= lo) & (kk < lo + ATT_WINDOW + CHUNK)
        for h in range(n_heads):
            bias_ref[h] = jnp.where(band, _toeplitz_bias(tvec_ref, h, PAIR)[:, :PAIR_KEYS], NEG_INF)

    kbuf[0:blk, :] = kp_ref[...]
    kbuf[blk:2 * blk, :] = kc_ref[...]
    vbuf[0:blk, :] = vp_ref[...]
    vbuf[blk:2 * blk, :] = vc_ref[...]

    scale = head_dim ** -0.5
    key_idx = lax.broadcasted_iota(jnp.int32, (1, PAIR_KEYS), 1)
    upper = lax.broadcasted_iota(jnp.int32, (PAIR, 2 * head_dim), 1) < head_dim

    def pair_body(p, carry):
        r0 = pl.multiple_of(p * PAIR, PAIR)
        missing = (qb == 0) & (key_idx < blk - r0)
        keymask = jnp.where(missing, NEG_INF, 0.0)
        for h in range(0, n_heads, 2):
            c0 = h * head_dim
            q2 = q_ref[pl.ds(r0, PAIR), c0:c0 + 2 * head_dim]
            k2 = kbuf[pl.ds(r0, PAIR_KEYS), c0:c0 + 2 * head_dim]
            v2 = vbuf[pl.ds(r0, PAIR_KEYS), c0:c0 + 2 * head_dim]
            zero = jnp.zeros_like(q2)
            qbd = jnp.concatenate([jnp.where(upper, q2, zero), jnp.where(upper, zero, q2)], axis=0)
            s = lax.dot_general(qbd, k2, (((1,), (1,)), ((), ())), preferred_element_type=F32)
            bias = bias_ref[h:h + 2].reshape(2 * PAIR, PAIR_KEYS)
            s = s * scale + bias + keymask
            e = jnp.exp(s - jnp.max(s, axis=-1, keepdims=True))
            l = jnp.sum(e, axis=-1, keepdims=True)
            o = jnp.dot(e.astype(BF16), v2, preferred_element_type=F32)
            y_ref[pl.ds(r0, PAIR), c0:c0 + head_dim] = o[:PAIR, :head_dim] / l[:PAIR]
            y_ref[pl.ds(r0, PAIR), c0 + head_dim:c0 + 2 * head_dim] = o[PAIR:, head_dim:] / l[PAIR:]
        return carry

    lax.fori_loop(0, blk // PAIR, pair_body, 0)
    o_ref[...] = _rms(y_ref[...], ga_ref[...]).astype(BF16)


def _attn_prompt(q, k, v, tvec, g_attn, *, layer, n_seq, seq_len, n_heads, head_dim):
    t, d_attn = q.shape
    blk = ATT_WINDOW
    assert seq_len % blk == 0 and blk % PAIR == 0
    nb = seq_len // blk
    cur = pl.BlockSpec((blk, d_attn), lambda b, i: (b * nb + i, 0))
    prev = pl.BlockSpec((blk, d_attn), lambda b, i: (b * nb + jnp.maximum(i - 1, 0), 0))
    kern = functools.partial(_attn_prompt_kernel, n_heads=n_heads, head_dim=head_dim, blk=blk)
    return pl.pallas_call(
        kern,
        grid=(n_seq, nb),
        in_specs=[cur, prev, cur, prev, cur,
                  _resident(layer, tvec.shape[1:]), _resident(layer, (1, d_attn))],
        out_specs=cur,
        out_shape=jax.ShapeDtypeStruct((t, d_attn), BF16),
        scratch_shapes=[
            pltpu.VMEM((2 * blk, d_attn), BF16),
            pltpu.VMEM((2 * blk, d_attn), BF16),
            pltpu.VMEM((n_heads, PAIR, PAIR_KEYS), F32),
            pltpu.VMEM((blk, d_attn), F32),
        ],
        compiler_params=_params(("arbitrary", "arbitrary")),
        name="attn_prompt",
    )(q, k, k, v, v, tvec, g_attn)


def _attn_sample_kernel(q_ref, kn_ref, vn_ref, kc_ref, vc_ref, tvec_ref, ga_ref, o_ref, y_ref,
                        *, n_heads, head_dim, n_new, n_cache):
    scale = head_dim ** -0.5
    for h in range(n_heads):
        c0 = h * head_dim
        bias = _toeplitz_bias(tvec_ref, h, n_new)
        q = q_ref[:, c0:c0 + head_dim]
        kc = kc_ref[:, c0:c0 + head_dim].astype(BF16)
        vc = vc_ref[:, c0:c0 + head_dim].astype(BF16)
        kn = kn_ref[:, c0:c0 + head_dim]
        vn = vn_ref[:, c0:c0 + head_dim]
        nt = (((1,), (1,)), ((), ()))
        sc = lax.dot_general(q, kc, nt, preferred_element_type=F32) * scale + bias[:, :n_cache]
        sn = (lax.dot_general(q, kn, nt, preferred_element_type=F32) * scale
              + bias[:, n_cache:n_cache + n_new])
        m = jnp.maximum(jnp.max(sc, axis=-1, keepdims=True), jnp.max(sn, axis=-1, keepdims=True))
        ec = jnp.exp(sc - m)
        en = jnp.exp(sn - m)
        l = jnp.sum(ec, axis=-1, keepdims=True) + jnp.sum(en, axis=-1, keepdims=True)
        o = (jnp.dot(ec.astype(BF16), vc, preferred_element_type=F32)
             + jnp.dot(en.astype(BF16), vn, preferred_element_type=F32))
        y_ref[:, c0:c0 + head_dim] = o / l
    o_ref[...] = _rms(y_ref[...], ga_ref[...]).astype(BF16)


def _attn_sample(q, k, v, cache_k, cache_v, tvec, g_attn, *, layer, n_seq, n_new, n_heads, head_dim):
    t, d_attn = q.shape
    n_cache = cache_k.shape[2]
    assert n_cache == ATT_WINDOW and n_new <= PAIR
    new = pl.BlockSpec((n_new, d_attn), lambda b: (b, 0))
    cache = pl.BlockSpec((None, None, n_cache, d_attn), lambda b: (layer, b, 0, 0))
    kern = functools.partial(_attn_sample_kernel, n_heads=n_heads, head_dim=head_dim,
                             n_new=n_new, n_cache=n_cache)
    return pl.pallas_call(
        kern,
        grid=(n_seq,),
        in_specs=[new, new, new, cache, cache,
                  _resident(layer, tvec.shape[1:]), _resident(layer, (1, d_attn))],
        out_specs=new,
        out_shape=jax.ShapeDtypeStruct((t, d_attn), BF16),
        scratch_shapes=[pltpu.VMEM((n_new, d_attn), F32)],
        compiler_params=_params(("arbitrary",)),
        name="attn_sample",
    )(q, k, v, cache_k, cache_v, tvec, g_attn)


def _mix_out_kernel(mc_ref, ma_ref, wo_ref, gpost_ref, h_ref, o_ref, *, d_conv, rc):
    for r in range(0, h_ref.shape[0], rc):
        z = (jnp.dot(mc_ref[r:r + rc, :], wo_ref[0:d_conv, :], preferred_element_type=F32)
             + jnp.dot(ma_ref[r:r + rc, :], wo_ref[d_conv:, :], preferred_element_type=F32))
        o_ref[r:r + rc, :] = h_ref[r:r + rc, :] + _rms(z, gpost_ref[...])


def _mix_out(mix_c, mix_a, w_out, g_post, h, *, layer, tm):
    t, d = h.shape
    d_conv = mix_c.shape[1]
    d_attn = mix_a.shape[1]
    tile = lambda n: pl.BlockSpec((tm, n), lambda i: (i, 0))
    return pl.pallas_call(
        functools.partial(_mix_out_kernel, d_conv=d_conv, rc=min(tm, MXU_DIM)),
        grid=(t // tm,),
        in_specs=[tile(d_conv), tile(d_attn), _resident(layer, w_out.shape[1:]),
                  _resident(layer, (1, d)), tile(d)],
        out_specs=tile(d),
        out_shape=jax.ShapeDtypeStruct((t, d), F32),
        compiler_params=_params(("arbitrary",)),
        name="mix_out",
    )(mix_c, mix_a, w_out, g_post, h)


def _pad_axis(w, axis, mult):
    pad = [(0, 0)] * w.ndim
    pad[axis] = (0, -w.shape[axis] % mult)
    return jnp.pad(w, pad)


def _bias_vectors(table):
    depth, n_rel, heads = table.shape
    clip = (n_rel - 1) // 2
    far = table[:, 2 * clip:2 * clip + 1, :]
    n_far = ATT_WINDOW - clip + 1
    ramp = table[:, 2 * clip - 1:0:-1, :]
    tail = BIAS_LANES - n_far - ramp.shape[1]
    t = jnp.concatenate([jnp.broadcast_to(far, (depth, n_far, heads)), ramp,
                         jnp.broadcast_to(far, (depth, tail, heads))], axis=1)
    return jnp.swapaxes(t, 1, 2).astype(F32)


def kernel(x_prompt, x_sample, cache_k, cache_v, state_conv, ln_ffn1_pre, ffn1_w_gate, ffn1_w_up,
           ffn1_w_down, ln_ffn1_post, ln_mix_pre, w_in, conv_w, rel_bias, g_conv_out, g_attn_out, w_out,
           ln_mix_post, ln_ffn2_pre, ffn2_w_gate, ffn2_w_up, ffn2_w_down, ln_ffn2_post):
    batch, seq, d = x_prompt.shape
    dec_batch, dec_seq, _ = x_sample.shape
    depth = w_in.shape[0]
    n_cache, n_heads, head_dim = cache_k.shape[2:]
    d_attn = n_heads * head_dim
    d_conv = conv_w.shape[2]
    keep = min(ATT_WINDOW, seq)

    tf = 2 * MXU_DIM
    tm_s = dec_batch * dec_seq
    assert n_heads % 2 == 0
    tiles = dict(prompt=dict(ffn=1024, proj=256, mix=1024), sample=dict(ffn=tm_s, proj=tm_s, mix=tm_s))

    gain = lambda a: a.reshape(depth, 1, -1)
    ffn1 = (gain(ln_ffn1_pre), _pad_axis(ffn1_w_gate.astype(BF16), 2, tf), _pad_axis(ffn1_w_up.astype(BF16), 2, tf),
            _pad_axis(ffn1_w_down.astype(BF16), 1, tf), gain(ln_ffn1_post))
    ffn2 = (gain(ln_ffn2_pre), _pad_axis(ffn2_w_gate.astype(BF16), 2, tf), _pad_axis(ffn2_w_up.astype(BF16), 2, tf),
            _pad_axis(ffn2_w_down.astype(BF16), 1, tf), gain(ln_ffn2_post))
    w_in_b = w_in.astype(BF16)
    w_out_b = w_out.astype(BF16)
    tvec = _bias_vectors(rel_bias)
    cache_k2 = cache_k.reshape(depth, dec_batch, n_cache, d_attn)
    cache_v2 = cache_v.reshape(depth, dec_batch, n_cache, d_attn)
    zero_state = jnp.zeros((1, batch, 2, d_conv), F32)

    def layer(l, h, tile, seq_len, conv_init, init_layer, attend):
        h = _ffn(h, *ffn1, layer=l, tm=tile["ffn"], tf=tf)
        q, k, v, mix_c, conv_new, k_keep, v_keep = _proj(
            h, gain(ln_mix_pre), w_in_b, conv_w, conv_init, gain(g_conv_out),
            layer=l, init_layer=init_layer, tm=tile["proj"], seq_len=seq_len, d_attn=d_attn)
        mix_a = attend(q, k, v)
        h = _mix_out(mix_c, mix_a, w_out_b, gain(ln_mix_post), h, layer=l, tm=tile["mix"])
        h = _ffn(h, *ffn2, layer=l, tm=tile["ffn"], tf=tf)
        return h, conv_new, k_keep, v_keep

    hp = x_prompt.reshape(batch * seq, d)
    hs = x_sample.reshape(dec_batch * dec_seq, d)
    outs = {n: [] for n in ("kp", "vp", "cp", "ks", "vs", "cs")}
    for l in range(depth):
        attend_prompt = functools.partial(
            _attn_prompt, tvec=tvec, g_attn=gain(g_attn_out), layer=l, n_seq=batch, seq_len=seq,
            n_heads=n_heads, head_dim=head_dim)
        attend_sample = functools.partial(
            _attn_sample, cache_k=cache_k2, cache_v=cache_v2, tvec=tvec, g_attn=gain(g_attn_out), layer=l,
            n_seq=dec_batch, n_new=dec_seq, n_heads=n_heads, head_dim=head_dim)
        hp, c_p, k_p, v_p = layer(l, hp, tiles["prompt"], seq, zero_state, 0, attend_prompt)
        hs, c_s, k_s, v_s = layer(l, hs, tiles["sample"], dec_seq, state_conv, l, attend_sample)
        outs["kp"].append(k_p.reshape(batch, keep, n_heads, head_dim))
        outs["vp"].append(v_p.reshape(batch, keep, n_heads, head_dim))
        outs["cp"].append(c_p)
        outs["ks"].append(k_s.reshape(dec_batch, dec_seq, n_heads, head_dim))
        outs["vs"].append(v_s.reshape(dec_batch, dec_seq, n_heads, head_dim))
        outs["cs"].append(c_s)

    return (hp.reshape(batch, seq, d), hs.reshape(dec_batch, dec_seq, d),
            jnp.stack(outs["kp"]), jnp.stack(outs["vp"]), jnp.stack(outs["cp"]),
            jnp.stack(outs["ks"]), jnp.stack(outs["vs"]), jnp.stack(outs["cs"]))
```

```python
import functools

import jax
import jax.numpy as jnp
from jax import lax
from jax.experimental import pallas as pl
from jax.experimental.pallas import tpu as pltpu

F32 = jnp.float32
BF16 = jnp.bfloat16

CHUNK = 64
LEFT_CHUNKS = 8
ATT_WINDOW = LEFT_CHUNKS * CHUNK
EPS = 1e-6
NEG_INF = -1e30

MXU_DIM = 256
VMEM_LIMIT_BYTES = 60 * 1024 * 1024

PAIR = 2 * CHUNK
PAIR_KEYS = ATT_WINDOW + PAIR
BIAS_LANES = PAIR_KEYS + PAIR


def _rms(x, g):
    ms = jnp.mean(x * x, axis=-1, keepdims=True)
    return x * lax.rsqrt(ms + EPS) * g


def _params(semantics):
    return pltpu.CompilerParams(dimension_semantics=semantics, vmem_limit_bytes=VMEM_LIMIT_BYTES)


def _resident(layer, shape):
    return pl.BlockSpec((None,) + tuple(shape), lambda *_: (layer,) + (0,) * len(shape),
                        pipeline_mode=pl.Buffered(1))


def _ffn_kernel(x_ref, gpre_ref, wg_ref, wu_ref, wd_ref, gpost_ref, o_ref, u_ref, *, n_chunks, last_tf, rc):
    j = pl.program_id(1)
    tm = x_ref.shape[0]
    tf = wg_ref.shape[1]

    def gate_up(u, n):
        g = jnp.dot(u, wg_ref[:, :n], preferred_element_type=F32)
        up = jnp.dot(u, wu_ref[:, :n], preferred_element_type=F32)
        return (g * jax.nn.sigmoid(g) * up).astype(BF16)

    def down(h, n):
        return jnp.dot(h, wd_ref[:n, :], preferred_element_type=F32)

    @pl.when(j == 0)
    def _():
        for r in range(0, tm, rc):
            u = _rms(x_ref[r:r + rc, :], gpre_ref[...]).astype(BF16)
            u_ref[r:r + rc, :] = u
            o_ref[r:r + rc, :] = down(gate_up(u, tf), tf)

    @pl.when((j > 0) & (j < n_chunks - 1))
    def _():
        o_ref[...] += down(gate_up(u_ref[...], tf), tf)

    @pl.when(j == n_chunks - 1)
    def _():
        h = gate_up(u_ref[...], last_tf)
        for r in range(0, tm, rc):
            acc = o_ref[r:r + rc, :] + down(h[r:r + rc, :], last_tf)
            o_ref[r:r + rc, :] = x_ref[r:r + rc, :] + 0.5 * _rms(acc, gpost_ref[...])


def _ffn(x, g_pre, wg, wu, wd, g_post, *, layer, tm, tf):
    t, d = x.shape
    f = wg.shape[2]
    n_chunks = pl.cdiv(f, tf)
    last_tf = f - (n_chunks - 1) * tf
    assert t % tm == 0 and n_chunks >= 2
    rc = min(tm, MXU_DIM)
    gain = pl.BlockSpec((None, 1, d), lambda i, j: (layer, 0, 0))
    return pl.pallas_call(
        functools.partial(_ffn_kernel, n_chunks=n_chunks, last_tf=last_tf, rc=rc),
        grid=(t // tm, n_chunks),
        in_specs=[
            pl.BlockSpec((tm, d), lambda i, j: (i, 0)),
            gain,
            pl.BlockSpec((None, d, tf), lambda i, j: (layer, 0, j)),
            pl.BlockSpec((None, d, tf), lambda i, j: (layer, 0, j)),
            pl.BlockSpec((None, tf, d), lambda i, j: (layer, j, 0)),
            gain,
        ],
        out_specs=pl.BlockSpec((tm, d), lambda i, j: (i, 0)),
        out_shape=jax.ShapeDtypeStruct((t, d), F32),
        scratch_shapes=[pltpu.VMEM((tm, d), BF16)],
        compiler_params=_params(("arbitrary", "arbitrary")),
        name="ffn",
    )(x, g_pre, wg, wu, wd, g_post)


def _proj_kernel(h_ref, gpre_ref, win_ref, convw_ref, cinit_ref, gc_ref, *rest,
                 tm, seq_len, d_conv, d_attn, tail_tiles):
    q_ref, k_ref, v_ref, mixc_ref, cst_ref, kt_ref, vt_ref, carry_ref = rest[-8:]
    u = _rms(h_ref[...], gpre_ref[...]).astype(BF16)

    def proj(c0, n):
        return jnp.dot(u, win_ref[:, c0:c0 + n], preferred_element_type=F32)

    bg = proj(0, d_conv)
    gx = proj(d_conv, d_conv) * proj(2 * d_conv, d_conv)

    g1 = pltpu.roll(gx, 1, axis=0)
    g2 = pltpu.roll(gx, 2, axis=0)
    row = lax.broadcasted_iota(jnp.int32, (tm, 1), 0)
    if seq_len >= tm:
        ti = pl.program_id(0) % (seq_len // tm)
        first = ti == 0
        halo0 = jnp.where(first, cinit_ref[0, 0:1, :], carry_ref[0:1, :])
        halo1 = jnp.where(first, cinit_ref[0, 1:2, :], carry_ref[1:2, :])
        pos = row
        carry_ref[...] = gx[tm - 2:tm, :]
        cst_ref[0] = gx[tm - 2:tm, :]
    else:
        n_seq = tm // seq_len
        halo0 = jnp.broadcast_to(cinit_ref[:, 0:1, :], (n_seq, seq_len, d_conv)).reshape(tm, d_conv)
        halo1 = jnp.broadcast_to(cinit_ref[:, 1:2, :], (n_seq, seq_len, d_conv)).reshape(tm, d_conv)
        pos = row % seq_len
        cst_ref[...] = gx.reshape(n_seq, seq_len, d_conv)[:, seq_len - 2:seq_len, :]
    g1 = jnp.where(pos == 0, halo1, g1)
    g2 = jnp.where(pos == 0, halo0, jnp.where(pos == 1, halo1, g2))
    w = convw_ref[...]
    y = bg * (g2 * w[0:1, :] + g1 * w[1:2, :] + gx * w[2:3, :])
    mixc_ref[...] = _rms(y, gc_ref[...]).astype(BF16)

    q_ref[...] = proj(3 * d_conv, d_attn).astype(BF16)
    k = proj(3 * d_conv + d_attn, d_attn)
    v = proj(3 * d_conv + 2 * d_attn, d_attn)
    k_ref[...] = k.astype(BF16)
    v_ref[...] = v.astype(BF16)

    if seq_len >= tm:
        @pl.when(ti >= seq_len // tm - tail_tiles)
        def _():
            kt_ref[...] = k
            vt_ref[...] = v
    else:
        kt_ref[...] = k
        vt_ref[...] = v


def _proj(h, g_pre, w_in, conv_w, conv_init, g_conv, kv_keep, *, layer, init_layer, tm, seq_len, d_attn):
    t, d = h.shape
    depth, _, d_conv = conv_w.shape
    n_seq = t // seq_len
    keep = min(ATT_WINDOW, seq_len)
    if seq_len >= tm:
        assert seq_len % tm == 0 and keep % tm == 0
        tiles_per_seq = seq_len // tm
        tail_tiles = keep // tm
        seq_of = lambda i: i // tiles_per_seq
        state_block = (1, 2, d_conv)
        state_map = lambda i: (seq_of(i), 0, 0)
        tail_of = lambda i: (seq_of(i) * tail_tiles
                             + jnp.maximum(i % tiles_per_seq - (tiles_per_seq - tail_tiles), 0))
    else:
        assert tm % seq_len == 0 and keep == seq_len
        tail_tiles = 1
        state_block = (tm // seq_len, 2, d_conv)
        state_map = lambda i: (i, 0, 0)
        tail_of = lambda i: i
    tail_blocks = n_seq * keep // tm
    tail_spec = pl.BlockSpec((tm, d_attn), lambda i: (layer * tail_blocks + tail_of(i), 0))
    tile = lambda n: pl.BlockSpec((tm, n), lambda i: (i, 0))
    kern = functools.partial(_proj_kernel, tm=tm, seq_len=seq_len, d_conv=d_conv, d_attn=d_attn,
                             tail_tiles=tail_tiles)
    in_specs = [
        tile(d),
        _resident(layer, (1, d)),
        _resident(layer, w_in.shape[1:]),
        _resident(layer, conv_w.shape[1:]),
        pl.BlockSpec((None,) + state_block, lambda i: (init_layer,) + state_map(i)),
        _resident(layer, (1, d_conv)),
    ]
    args = [h, g_pre, w_in, conv_w, conv_init, g_conv]
    aliases = {}
    if kv_keep is not None:
        aliases = {len(args): 5, len(args) + 1: 6}
        in_specs += [pl.BlockSpec(memory_space=pl.ANY)] * 2
        args += list(kv_keep)
    cache_shape = jax.ShapeDtypeStruct((depth * n_seq * keep, d_attn), F32)
    return pl.pallas_call(
        kern,
        grid=(t // tm,),
        in_specs=in_specs,
        out_specs=[
            tile(d_attn), tile(d_attn), tile(d_attn), tile(d_conv),
            pl.BlockSpec(state_block, state_map),
            tail_spec, tail_spec,
        ],
        out_shape=[
            jax.ShapeDtypeStruct((t, d_attn), BF16),
            jax.ShapeDtypeStruct((t, d_attn), BF16),
            jax.ShapeDtypeStruct((t, d_attn), BF16),
            jax.ShapeDtypeStruct((t, d_conv), BF16),
            jax.ShapeDtypeStruct((n_seq, 2, d_conv), F32),
            cache_shape, cache_shape,
        ],
        input_output_aliases=aliases,
        scratch_shapes=[pltpu.VMEM((2, d_conv), F32)],
        compiler_params=_params(("arbitrary",)),
        name="proj",
    )(*args)


def _toeplitz_bias(tvec_ref, h, rows):
    t = jnp.broadcast_to(tvec_ref[h:h + 1, :], (rows, BIAS_LANES))
    return pltpu.roll(t, 0, axis=1, stride=1, stride_axis=0)


def _attn_prompt_kernel(q_ref, kp_ref, kc_ref, vp_ref, vc_ref, tvec_ref, ga_ref, o_ref,
                        kbuf, vbuf, bias_ref, y_ref, *, n_heads, head_dim, blk):
    qb = pl.program_id(1)

    @pl.when((pl.program_id(0) == 0) & (qb == 0))
    def _():
        qi = lax.broadcasted_iota(jnp.int32, (PAIR, PAIR_KEYS), 0)
        kk = lax.broadcasted_iota(jnp.int32, (PAIR, PAIR_KEYS), 1)
        lo = (qi // CHUNK) * CHUNK
        band = (kk >= lo) & (kk < lo + ATT_WINDOW + CHUNK)
        for h in range(n_heads):
            bias_ref[h] = jnp.where(band, _toeplitz_bias(tvec_ref, h, PAIR)[:, :PAIR_KEYS], NEG_INF)

    kbuf[0:blk, :] = kp_ref[...]
    kbuf[blk:2 * blk, :] = kc_ref[...]
    vbuf[0:blk, :] = vp_ref[...]
    vbuf[blk:2 * blk, :] = vc_ref[...]

    scale = head_dim ** -0.5
    key_idx = lax.broadcasted_iota(jnp.int32, (1, PAIR_KEYS), 1)
    upper = lax.broadcasted_iota(jnp.int32, (PAIR, 2 * head_dim), 1) < head_dim

    def pair_body(p, carry):
        r0 = pl.multiple_of(p * PAIR, PAIR)
        missing = (qb == 0) & (key_idx < blk - r0)
        keymask = jnp.where(missing, NEG_INF, 0.0)
        for h in range(0, n_heads, 2):
            c0 = h * head_dim
            q2 = q_ref[pl.ds(r0, PAIR), c0:c0 + 2 * head_dim]
            k2 = kbuf[pl.ds(r0, PAIR_KEYS), c0:c0 + 2 * head_dim]
            v2 = vbuf[pl.ds(r0, PAIR_KEYS), c0:c0 + 2 * head_dim]
            zero = jnp.zeros_like(q2)
            qbd = jnp.concatenate([jnp.where(upper, q2, zero), jnp.where(upper, zero, q2)], axis=0)
            s = lax.dot_general(qbd, k2, (((1,), (1,)), ((), ())), preferred_element_type=F32)
            bias = bias_ref[h:h + 2].reshape(2 * PAIR, PAIR_KEYS)
            s = s * scale + bias + keymask
            e = jnp.exp(s - jnp.max(s, axis=-1, keepdims=True))
            l = jnp.sum(e, axis=-1, keepdims=True)
            o = jnp.dot(e.astype(BF16), v2, preferred_element_type=F32)
            y_ref[pl.ds(r0, PAIR), c0:c0 + head_dim] = o[:PAIR, :head_dim] / l[:PAIR]
            y_ref[pl.ds(r0, PAIR), c0 + head_dim:c0 + 2 * head_dim] = o[PAIR:, head_dim:] / l[PAIR:]
        return carry

    lax.fori_loop(0, blk // PAIR, pair_body, 0, unroll=True)
    o_ref[...] = _rms(y_ref[...], ga_ref[...]).astype(BF16)


def _attn_prompt(q, k, v, tvec, g_attn, *, layer, n_seq, seq_len, n_heads, head_dim):
    t, d_attn = q.shape
    blk = ATT_WINDOW
    assert seq_len % blk == 0 and blk % PAIR == 0
    nb = seq_len // blk
    cur = pl.BlockSpec((blk, d_attn), lambda b, i: (b * nb + i, 0))
    prev = pl.BlockSpec((blk, d_attn), lambda b, i: (b * nb + jnp.maximum(i - 1, 0), 0))
    kern = functools.partial(_attn_prompt_kernel, n_heads=n_heads, head_dim=head_dim, blk=blk)
    return pl.pallas_call(
        kern,
        grid=(n_seq, nb),
        in_specs=[cur, prev, cur, prev, cur,
                  _resident(layer, tvec.shape[1:]), _resident(layer, (1, d_attn))],
        out_specs=cur,
        out_shape=jax.ShapeDtypeStruct((t, d_attn), BF16),
        scratch_shapes=[
            pltpu.VMEM((2 * blk, d_attn), BF16),
            pltpu.VMEM((2 * blk, d_attn), BF16),
            pltpu.VMEM((n_heads, PAIR, PAIR_KEYS), F32),
            pltpu.VMEM((blk, d_attn), F32),
        ],
        compiler_params=_params(("arbitrary", "arbitrary")),
        name="attn_prompt",
    )(q, k, k, v, v, tvec, g_attn)


def _attn_sample_kernel(q_ref, kn_ref, vn_ref, kc_ref, vc_ref, tvec_ref, ga_ref, o_ref, y_ref,
                        *, n_heads, head_dim, n_new, n_cache):
    scale = head_dim ** -0.5
    for h in range(n_heads):
        c0 = h * head_dim
        bias = _toeplitz_bias(tvec_ref, h, n_new)
        q = q_ref[:, c0:c0 + head_dim]
        kc = kc_ref[:, h, :].astype(BF16)
        vc = vc_ref[:, h, :].astype(BF16)
        kn = kn_ref[:, c0:c0 + head_dim]
        vn = vn_ref[:, c0:c0 + head_dim]
        nt = (((1,), (1,)), ((), ()))
        sc = lax.dot_general(q, kc, nt, preferred_element_type=F32) * scale + bias[:, :n_cache]
        sn = (lax.dot_general(q, kn, nt, preferred_element_type=F32) * scale
              + bias[:, n_cache:n_cache + n_new])
        m = jnp.maximum(jnp.max(sc, axis=-1, keepdims=True), jnp.max(sn, axis=-1, keepdims=True))
        ec = jnp.exp(sc - m)
        en = jnp.exp(sn - m)
        l = jnp.sum(ec, axis=-1, keepdims=True) + jnp.sum(en, axis=-1, keepdims=True)
        o = (jnp.dot(ec.astype(BF16), vc, preferred_element_type=F32)
             + jnp.dot(en.astype(BF16), vn, preferred_element_type=F32))
        y_ref[:, c0:c0 + head_dim] = o / l
    o_ref[...] = _rms(y_ref[...], ga_ref[...]).astype(BF16)


def _attn_sample(q, k, v, cache_k, cache_v, tvec, g_attn, *, layer, n_seq, n_new, n_heads, head_dim):
    t, d_attn = q.shape
    n_cache = cache_k.shape[2]
    assert n_cache == ATT_WINDOW and n_new <= PAIR
    new = pl.BlockSpec((n_new, d_attn), lambda b: (b, 0))
    cache = pl.BlockSpec((None, None, n_cache, n_heads, head_dim), lambda b: (layer, b, 0, 0, 0))
    kern = functools.partial(_attn_sample_kernel, n_heads=n_heads, head_dim=head_dim,
                             n_new=n_new, n_cache=n_cache)
    return pl.pallas_call(
        kern,
        grid=(n_seq,),
        in_specs=[new, new, new, cache, cache,
                  _resident(layer, tvec.shape[1:]), _resident(layer, (1, d_attn))],
        out_specs=new,
        out_shape=jax.ShapeDtypeStruct((t, d_attn), BF16),
        scratch_shapes=[pltpu.VMEM((n_new, d_attn), F32)],
        compiler_params=_params(("arbitrary",)),
        name="attn_sample",
    )(q, k, v, cache_k, cache_v, tvec, g_attn)


def _mix_out_kernel(mc_ref, ma_ref, wo_ref, gpost_ref, h_ref, o_ref, *, d_conv, rc):
    for r in range(0, h_ref.shape[0], rc):
        z = (jnp.dot(mc_ref[r:r + rc, :], wo_ref[0:d_conv, :], preferred_element_type=F32)
             + jnp.dot(ma_ref[r:r + rc, :], wo_ref[d_conv:, :], preferred_element_type=F32))
        o_ref[r:r + rc, :] = h_ref[r:r + rc, :] + _rms(z, gpost_ref[...])


def _mix_out(mix_c, mix_a, w_out, g_post, h, *, layer, tm):
    t, d = h.shape
    d_conv = mix_c.shape[1]
    d_attn = mix_a.shape[1]
    tile = lambda n: pl.BlockSpec((tm, n), lambda i: (i, 0))
    return pl.pallas_call(
        functools.partial(_mix_out_kernel, d_conv=d_conv, rc=min(tm, MXU_DIM)),
        grid=(t // tm,),
        in_specs=[tile(d_conv), tile(d_attn), _resident(layer, w_out.shape[1:]),
                  _resident(layer, (1, d)), tile(d)],
        out_specs=tile(d),
        out_shape=jax.ShapeDtypeStruct((t, d), F32),
        compiler_params=_params(("arbitrary",)),
        name="mix_out",
    )(mix_c, mix_a, w_out, g_post, h)


def _pad_axis(w, axis, mult):
    pad = [(0, 0)] * w.ndim
    pad[axis] = (0, -w.shape[axis] % mult)
    return jnp.pad(w, pad)


def _bias_vectors(table):
    depth, n_rel, heads = table.shape
    clip = (n_rel - 1) // 2
    far = table[:, 2 * clip:2 * clip + 1, :]
    n_far = ATT_WINDOW - clip + 1
    ramp = table[:, 2 * clip - 1:0:-1, :]
    tail = BIAS_LANES - n_far - ramp.shape[1]
    t = jnp.concatenate([jnp.broadcast_to(far, (depth, n_far, heads)), ramp,
                         jnp.broadcast_to(far, (depth, tail, heads))], axis=1)
    return jnp.swapaxes(t, 1, 2).astype(F32)


def kernel(x_prompt, x_sample, cache_k, cache_v, state_conv, ln_ffn1_pre, ffn1_w_gate, ffn1_w_up,
           ffn1_w_down, ln_ffn1_post, ln_mix_pre, w_in, conv_w, rel_bias, g_conv_out, g_attn_out, w_out,
           ln_mix_post, ln_ffn2_pre, ffn2_w_gate, ffn2_w_up, ffn2_w_down, ln_ffn2_post):
    batch, seq, d = x_prompt.shape
    dec_batch, dec_seq, _ = x_sample.shape
    depth = w_in.shape[0]
    n_cache, n_heads, head_dim = cache_k.shape[2:]
    d_attn = n_heads * head_dim
    d_conv = conv_w.shape[2]
    keep = min(ATT_WINDOW, seq)

    tf = 2 * MXU_DIM
    tm_s = dec_batch * dec_seq
    assert n_heads % 2 == 0
    tiles = dict(prompt=dict(ffn=1024, proj=256, mix=1024), sample=dict(ffn=tm_s, proj=tm_s, mix=tm_s))

    gain = lambda a: a.reshape(depth, 1, -1)
    ffn1 = (gain(ln_ffn1_pre), ffn1_w_gate.astype(BF16), ffn1_w_up.astype(BF16), ffn1_w_down.astype(BF16),
            gain(ln_ffn1_post))
    ffn2 = (gain(ln_ffn2_pre), ffn2_w_gate.astype(BF16), ffn2_w_up.astype(BF16), ffn2_w_down.astype(BF16),
            gain(ln_ffn2_post))
    w_in_b = w_in.astype(BF16)
    w_out_b = w_out.astype(BF16)
    tvec = _bias_vectors(rel_bias)
    zero_state = jnp.zeros((1, batch, 2, d_conv), F32)

    def layer(l, h, tile, seq_len, conv_init, init_layer, kv_keep, attend):
        h = _ffn(h, *ffn1, layer=l, tm=tile["ffn"], tf=tf)
        q, k, v, mix_c, conv_new, k_keep, v_keep = _proj(
            h, gain(ln_mix_pre), w_in_b, conv_w, conv_init, gain(g_conv_out), kv_keep,
            layer=l, init_layer=init_layer, tm=tile["proj"], seq_len=seq_len, d_attn=d_attn)
        mix_a = attend(q, k, v)
        h = _mix_out(mix_c, mix_a, w_out_b, gain(ln_mix_post), h, layer=l, tm=tile["mix"])
        h = _ffn(h, *ffn2, layer=l, tm=tile["ffn"], tf=tf)
        return h, conv_new, (k_keep, v_keep)

    hp = x_prompt.reshape(batch * seq, d)
    hs = x_sample.reshape(dec_batch * dec_seq, d)
    kv_p = kv_s = None
    conv_p, conv_s = [], []
    for l in range(depth):
        attend_prompt = functools.partial(
            _attn_prompt, tvec=tvec, g_attn=gain(g_attn_out), layer=l, n_seq=batch, seq_len=seq,
            n_heads=n_heads, head_dim=head_dim)
        attend_sample = functools.partial(
            _attn_sample, cache_k=cache_k, cache_v=cache_v, tvec=tvec, g_attn=gain(g_attn_out), layer=l,
            n_seq=dec_batch, n_new=dec_seq, n_heads=n_heads, head_dim=head_dim)
        hp, c_p, kv_p = layer(l, hp, tiles["prompt"], seq, zero_state, 0, kv_p, attend_prompt)
        hs, c_s, kv_s = layer(l, hs, tiles["sample"], dec_seq, state_conv, l, kv_s, attend_sample)
        conv_p.append(c_p)
        conv_s.append(c_s)

    return (hp.reshape(batch, seq, d), hs.reshape(dec_batch, dec_seq, d),
            kv_p[0].reshape(depth, batch, keep, n_heads, head_dim),
            kv_p[1].reshape(depth, batch, keep, n_heads, head_dim),
            jnp.stack(conv_p),
            kv_s[0].reshape(depth, dec_batch, dec_seq, n_heads, head_dim),
            kv_s[1].reshape(depth, dec_batch, dec_seq, n_heads, head_dim),
            jnp.stack(conv_s))
```

```python
import functools

import jax
import jax.numpy as jnp
from jax import lax
from jax.experimental import pallas as pl
from jax.experimental.pallas import tpu as pltpu

F32 = jnp.float32
BF16 = jnp.bfloat16

CHUNK = 64
LEFT_CHUNKS = 8
ATT_WINDOW = LEFT_CHUNKS * CHUNK
EPS = 1e-6
NEG_INF = -1e30

MXU_DIM = 256
VMEM_LIMIT_BYTES = 60 * 1024 * 1024

PAIR = 2 * CHUNK
PAIR_KEYS = ATT_WINDOW + PAIR
BIAS_LANES = PAIR_KEYS + PAIR


def _rms(x, g):
    ms = jnp.mean(x * x, axis=-1, keepdims=True)
    return x * lax.rsqrt(ms + EPS) * g


def _params(semantics):
    return pltpu.CompilerParams(dimension_semantics=semantics, vmem_limit_bytes=VMEM_LIMIT_BYTES)


def _resident(layer, shape):
    return pl.BlockSpec((None,) + tuple(shape), lambda *_: (layer,) + (0,) * len(shape),
                        pipeline_mode=pl.Buffered(1))


def _ffn_kernel(x_ref, gpre_ref, wg_ref, wu_ref, wd_ref, gpost_ref, o_ref, u_ref, *, n_chunks, last_tf, rc):
    j = pl.program_id(1)
    tm = x_ref.shape[0]
    tf = wg_ref.shape[1]

    def gate_up(u, n):
        g = jnp.dot(u, wg_ref[:, :n], preferred_element_type=F32)
        up = jnp.dot(u, wu_ref[:, :n], preferred_element_type=F32)
        return (g * jax.nn.sigmoid(g) * up).astype(BF16)

    def down(h, n):
        return jnp.dot(h, wd_ref[:n, :], preferred_element_type=F32)

    @pl.when(j == 0)
    def _():
        for r in range(0, tm, rc):
            u = _rms(x_ref[r:r + rc, :], gpre_ref[...]).astype(BF16)
            u_ref[r:r + rc, :] = u
            o_ref[r:r + rc, :] = down(gate_up(u, tf), tf)

    @pl.when((j > 0) & (j < n_chunks - 1))
    def _():
        o_ref[...] += down(gate_up(u_ref[...], tf), tf)

    @pl.when(j == n_chunks - 1)
    def _():
        h = gate_up(u_ref[...], last_tf)
        for r in range(0, tm, rc):
            acc = o_ref[r:r + rc, :] + down(h[r:r + rc, :], last_tf)
            o_ref[r:r + rc, :] = x_ref[r:r + rc, :] + 0.5 * _rms(acc, gpost_ref[...])


def _ffn(x, g_pre, wg, wu, wd, g_post, *, layer, tm, tf):
    t, d = x.shape
    f = wg.shape[2]
    n_chunks = pl.cdiv(f, tf)
    last_tf = f - (n_chunks - 1) * tf
    assert t % tm == 0 and n_chunks >= 2
    rc = min(tm, MXU_DIM)
    gain = pl.BlockSpec((None, 1, d), lambda i, j: (layer, 0, 0))
    return pl.pallas_call(
        functools.partial(_ffn_kernel, n_chunks=n_chunks, last_tf=last_tf, rc=rc),
        grid=(t // tm, n_chunks),
        in_specs=[
            pl.BlockSpec((tm, d), lambda i, j: (i, 0)),
            gain,
            pl.BlockSpec((None, d, tf), lambda i, j: (layer, 0, j)),
            pl.BlockSpec((None, d, tf), lambda i, j: (layer, 0, j)),
            pl.BlockSpec((None, tf, d), lambda i, j: (layer, j, 0)),
            gain,
        ],
        out_specs=pl.BlockSpec((tm, d), lambda i, j: (i, 0)),
        out_shape=jax.ShapeDtypeStruct((t, d), F32),
        scratch_shapes=[pltpu.VMEM((tm, d), BF16)],
        compiler_params=_params(("arbitrary", "arbitrary")),
        name="ffn",
    )(x, g_pre, wg, wu, wd, g_post)


def _proj_kernel(h_ref, gpre_ref, win_ref, convw_ref, cinit_ref, gc_ref, *rest,
                 tm, seq_len, d_conv, d_attn, tail_tiles):
    q_ref, k_ref, v_ref, mixc_ref, cst_ref, kt_ref, vt_ref, carry_ref = rest[-8:]
    u = _rms(h_ref[...], gpre_ref[...]).astype(BF16)

    def proj(c0, n):
        return jnp.dot(u, win_ref[:, c0:c0 + n], preferred_element_type=F32)

    bg = proj(0, d_conv)
    gx = proj(d_conv, d_conv) * proj(2 * d_conv, d_conv)

    g1 = pltpu.roll(gx, 1, axis=0)
    g2 = pltpu.roll(gx, 2, axis=0)
    row = lax.broadcasted_iota(jnp.int32, (tm, 1), 0)
    if seq_len >= tm:
        ti = pl.program_id(0) % (seq_len // tm)
        first = ti == 0
        halo0 = jnp.where(first, cinit_ref[0, 0:1, :], carry_ref[0:1, :])
        halo1 = jnp.where(first, cinit_ref[0, 1:2, :], carry_ref[1:2, :])
        pos = row
        carry_ref[...] = gx[tm - 2:tm, :]
        cst_ref[0] = gx[tm - 2:tm, :]
    else:
        n_seq = tm // seq_len
        halo0 = jnp.broadcast_to(cinit_ref[:, 0:1, :], (n_seq, seq_len, d_conv)).reshape(tm, d_conv)
        halo1 = jnp.broadcast_to(cinit_ref[:, 1:2, :], (n_seq, seq_len, d_conv)).reshape(tm, d_conv)
        pos = row % seq_len
        cst_ref[...] = gx.reshape(n_seq, seq_len, d_conv)[:, seq_len - 2:seq_len, :]
    g1 = jnp.where(pos == 0, halo1, g1)
    g2 = jnp.where(pos == 0, halo0, jnp.where(pos == 1, halo1, g2))
    w = convw_ref[...]
    y = bg * (g2 * w[0:1, :] + g1 * w[1:2, :] + gx * w[2:3, :])
    mixc_ref[...] = _rms(y, gc_ref[...]).astype(BF16)

    q_ref[...] = proj(3 * d_conv, d_attn).astype(BF16)
    k = proj(3 * d_conv + d_attn, d_attn)
    v = proj(3 * d_conv + 2 * d_attn, d_attn)
    k_ref[...] = k.astype(BF16)
    v_ref[...] = v.astype(BF16)

    if seq_len >= tm:
        @pl.when(ti >= seq_len // tm - tail_tiles)
        def _():
            kt_ref[...] = k
            vt_ref[...] = v
    else:
        kt_ref[...] = k
        vt_ref[...] = v


def _proj(h, g_pre, w_in, conv_w, conv_init, g_conv, kv_keep, *, layer, init_layer, tm, seq_len, d_attn):
    t, d = h.shape
    depth, _, d_conv = conv_w.shape
    n_seq = t // seq_len
    keep = min(ATT_WINDOW, seq_len)
    if seq_len >= tm:
        assert seq_len % tm == 0 and keep % tm == 0
        tiles_per_seq = seq_len // tm
        tail_tiles = keep // tm
        seq_of = lambda i: i // tiles_per_seq
        state_block = (1, 2, d_conv)
        state_map = lambda i: (seq_of(i), 0, 0)
        tail_of = lambda i: (seq_of(i) * tail_tiles
                             + jnp.maximum(i % tiles_per_seq - (tiles_per_seq - tail_tiles), 0))
    else:
        assert tm % seq_len == 0 and keep == seq_len
        tail_tiles = 1
        state_block = (tm // seq_len, 2, d_conv)
        state_map = lambda i: (i, 0, 0)
        tail_of = lambda i: i
    tail_blocks = n_seq * keep // tm
    tail_spec = pl.BlockSpec((tm, d_attn), lambda i: (layer * tail_blocks + tail_of(i), 0))
    tile = lambda n: pl.BlockSpec((tm, n), lambda i: (i, 0))
    kern = functools.partial(_proj_kernel, tm=tm, seq_len=seq_len, d_conv=d_conv, d_attn=d_attn,
                             tail_tiles=tail_tiles)
    in_specs = [
        tile(d),
        _resident(layer, (1, d)),
        _resident(layer, w_in.shape[1:]),
        _resident(layer, conv_w.shape[1:]),
        pl.BlockSpec((None,) + state_block, lambda i: (init_layer,) + state_map(i)),
        _resident(layer, (1, d_conv)),
    ]
    args = [h, g_pre, w_in, conv_w, conv_init, g_conv]
    aliases = {}
    if kv_keep is not None:
        aliases = {len(args): 5, len(args) + 1: 6}
        in_specs += [pl.BlockSpec(memory_space=pl.ANY)] * 2
        args += list(kv_keep)
    cache_shape = jax.ShapeDtypeStruct((depth * n_seq * keep, d_attn), F32)
    return pl.pallas_call(
        kern,
        grid=(t // tm,),
        in_specs=in_specs,
        out_specs=[
            tile(d_attn), tile(d_attn), tile(d_attn), tile(d_conv),
            pl.BlockSpec(state_block, state_map),
            tail_spec, tail_spec,
        ],
        out_shape=[
            jax.ShapeDtypeStruct((t, d_attn), BF16),
            jax.ShapeDtypeStruct((t, d_attn), BF16),
            jax.ShapeDtypeStruct((t, d_attn), BF16),
            jax.ShapeDtypeStruct((t, d_conv), BF16),
            jax.ShapeDtypeStruct((n_seq, 2, d_conv), F32),
            cache_shape, cache_shape,
        ],
        input_output_aliases=aliases,
        scratch_shapes=[pltpu.VMEM((2, d_conv), F32)],
        compiler_params=_params(("arbitrary",)),
        name="proj",
    )(*args)


def _toeplitz_bias(tvec_ref, h, rows):
    t = jnp.broadcast_to(tvec_ref[h:h + 1, :], (rows, BIAS_LANES))
    return pltpu.roll(t, 0, axis=1, stride=1, stride_axis=0)


def _attn_prompt_kernel(q_ref, kp_ref, kc_ref, vp_ref, vc_ref, tvec_ref, ga_ref, o_ref,
                        bias_ref, y_ref, *, n_heads, head_dim, blk):
    qb = pl.program_id(1)

    @pl.when((pl.program_id(0) == 0) & (qb == 0))
    def _():
        qi = lax.broadcasted_iota(jnp.int32, (PAIR, PAIR_KEYS), 0)
        kk = lax.broadcasted_iota(jnp.int32, (PAIR, PAIR_KEYS), 1)
        lo = (qi // CHUNK) * CHUNK
        band = (kk >= lo) & (kk < lo + ATT_WINDOW + CHUNK)
        for h in range(n_heads):
            bias_ref[h] = jnp.where(band, _toeplitz_bias(tvec_ref, h, PAIR)[:, :PAIR_KEYS], NEG_INF)

    scale = head_dim ** -0.5
    key_idx = lax.broadcasted_iota(jnp.int32, (1, PAIR_KEYS), 1)
    upper = lax.broadcasted_iota(jnp.int32, (PAIR, 2 * head_dim), 1) < head_dim

    for r0 in range(0, blk, PAIR):
        keymask = jnp.where((qb == 0) & (key_idx < blk - r0), NEG_INF, 0.0)
        for h in range(0, n_heads, 2):
            c0 = h * head_dim
            cols = slice(c0, c0 + 2 * head_dim)
            q2 = q_ref[r0:r0 + PAIR, cols]
            k2 = jnp.concatenate([kp_ref[r0:blk, cols], kc_ref[0:r0 + PAIR, cols]], axis=0)
            v2 = jnp.concatenate([vp_ref[r0:blk, cols], vc_ref[0:r0 + PAIR, cols]], axis=0)
            zero = jnp.zeros_like(q2)
            qbd = jnp.concatenate([jnp.where(upper, q2, zero), jnp.where(upper, zero, q2)], axis=0)
            s = lax.dot_general(qbd, k2, (((1,), (1,)), ((), ())), preferred_element_type=F32)
            bias = bias_ref[h:h + 2].reshape(2 * PAIR, PAIR_KEYS)
            s = s * scale + bias + keymask
            e = jnp.exp(s - jnp.max(s, axis=-1, keepdims=True))
            l = jnp.sum(e, axis=-1, keepdims=True)
            o = jnp.dot(e.astype(BF16), v2, preferred_element_type=F32)
            y_ref[r0:r0 + PAIR, c0:c0 + head_dim] = o[:PAIR, :head_dim] / l[:PAIR]
            y_ref[r0:r0 + PAIR, c0 + head_dim:c0 + 2 * head_dim] = o[PAIR:, head_dim:] / l[PAIR:]

    o_ref[...] = _rms(y_ref[...], ga_ref[...]).astype(BF16)


def _attn_prompt(q, k, v, tvec, g_attn, *, layer, n_seq, seq_len, n_heads, head_dim):
    t, d_attn = q.shape
    blk = ATT_WINDOW
    assert seq_len % blk == 0 and blk % PAIR == 0
    nb = seq_len // blk
    cur = pl.BlockSpec((blk, d_attn), lambda b, i: (b * nb + i, 0))
    prev = pl.BlockSpec((blk, d_attn), lambda b, i: (b * nb + jnp.maximum(i - 1, 0), 0))
    kern = functools.partial(_attn_prompt_kernel, n_heads=n_heads, head_dim=head_dim, blk=blk)
    return pl.pallas_call(
        kern,
        grid=(n_seq, nb),
        in_specs=[cur, prev, cur, prev, cur,
                  _resident(layer, tvec.shape[1:]), _resident(layer, (1, d_attn))],
        out_specs=cur,
        out_shape=jax.ShapeDtypeStruct((t, d_attn), BF16),
        scratch_shapes=[
            pltpu.VMEM((n_heads, PAIR, PAIR_KEYS), F32),
            pltpu.VMEM((blk, d_attn), F32),
        ],
        compiler_params=_params(("arbitrary", "arbitrary")),
        name="attn_prompt",
    )(q, k, k, v, v, tvec, g_attn)


def _attn_sample_kernel(q_ref, kn_ref, vn_ref, kc_ref, vc_ref, tvec_ref, tvec_rep_ref, ga_ref, o_ref,
                        bias_ref, y_ref, *, n_heads, head_dim, n_new, n_cache):
    cols = n_cache * n_heads

    @pl.when(pl.program_id(0) == 0)
    def _():
        col_head = lax.broadcasted_iota(jnp.int32, (n_new, cols), 1) % n_heads
        for h in range(n_heads):
            t = jnp.broadcast_to(tvec_rep_ref[h:h + 1, :], (n_new, tvec_rep_ref.shape[1]))
            shifted = pltpu.roll(t, 0, axis=1, stride=n_heads, stride_axis=0)[:, :cols]
            bias_ref[h * n_new:(h + 1) * n_new, :] = jnp.where(col_head == h, shifted, NEG_INF)

    scale = head_dim ** -0.5
    nt = (((1,), (1,)), ((), ()))
    head = lambda ref, h: ref[:, h * head_dim:(h + 1) * head_dim]
    q_all = jnp.concatenate([head(q_ref, h) for h in range(n_heads)], axis=0)
    sc = (lax.dot_general(q_all, kc_ref[...].astype(BF16), nt, preferred_element_type=F32) * scale
          + bias_ref[...])
    sn = jnp.concatenate(
        [lax.dot_general(head(q_ref, h), head(kn_ref, h), nt, preferred_element_type=F32) * scale
         + _toeplitz_bias(tvec_ref, h, n_new)[:, n_cache:n_cache + n_new] for h in range(n_heads)], axis=0)
    m = jnp.maximum(jnp.max(sc, axis=-1, keepdims=True), jnp.max(sn, axis=-1, keepdims=True))
    ec = jnp.exp(sc - m)
    en = jnp.exp(sn - m)
    l = jnp.sum(ec, axis=-1, keepdims=True) + jnp.sum(en, axis=-1, keepdims=True)
    o = jnp.dot(ec.astype(BF16), vc_ref[...].astype(BF16), preferred_element_type=F32)
    o = o + jnp.concatenate(
        [jnp.dot(en[h * n_new:(h + 1) * n_new, :].astype(BF16), head(vn_ref, h), preferred_element_type=F32)
         for h in range(n_heads)], axis=0)
    o = o / l
    for h in range(n_heads):
        y_ref[:, h * head_dim:(h + 1) * head_dim] = o[h * n_new:(h + 1) * n_new, :]
    o_ref[...] = _rms(y_ref[...], ga_ref[...]).astype(BF16)


def _attn_sample(q, k, v, cache_k, cache_v, tvec, tvec_rep, g_attn, *, layer, n_seq, n_new, n_heads, head_dim):
    t, d_attn = q.shape
    n_cache = cache_k.shape[2] // n_heads
    assert n_cache == ATT_WINDOW and n_new <= PAIR
    new = pl.BlockSpec((n_new, d_attn), lambda b: (b, 0))
    cache = pl.BlockSpec((None, None, n_cache * n_heads, head_dim), lambda b: (layer, b, 0, 0))
    kern = functools.partial(_attn_sample_kernel, n_heads=n_heads, head_dim=head_dim,
                             n_new=n_new, n_cache=n_cache)
    return pl.pallas_call(
        kern,
        grid=(n_seq,),
        in_specs=[new, new, new, cache, cache, _resident(layer, tvec.shape[1:]),
                  _resident(layer, tvec_rep.shape[1:]), _resident(layer, (1, d_attn))],
        out_specs=new,
        out_shape=jax.ShapeDtypeStruct((t, d_attn), BF16),
        scratch_shapes=[pltpu.VMEM((n_heads * n_new, n_cache * n_heads), F32),
                        pltpu.VMEM((n_new, d_attn), F32)],
        compiler_params=_params(("arbitrary",)),
        name="attn_sample",
    )(q, k, v, cache_k, cache_v, tvec, tvec_rep, g_attn)


def _mix_out_kernel(mc_ref, ma_ref, wo_ref, gpost_ref, h_ref, o_ref, *, d_conv, rc):
    for r in range(0, h_ref.shape[0], rc):
        z = (jnp.dot(mc_ref[r:r + rc, :], wo_ref[0:d_conv, :], preferred_element_type=F32)
             + jnp.dot(ma_ref[r:r + rc, :], wo_ref[d_conv:, :], preferred_element_type=F32))
        o_ref[r:r + rc, :] = h_ref[r:r + rc, :] + _rms(z, gpost_ref[...])


def _mix_out(mix_c, mix_a, w_out, g_post, h, *, layer, tm):
    t, d = h.shape
    d_conv = mix_c.shape[1]
    d_attn = mix_a.shape[1]
    tile = lambda n: pl.BlockSpec((tm, n), lambda i: (i, 0))
    return pl.pallas_call(
        functools.partial(_mix_out_kernel, d_conv=d_conv, rc=min(tm, MXU_DIM)),
        grid=(t // tm,),
        in_specs=[tile(d_conv), tile(d_attn), _resident(layer, w_out.shape[1:]),
                  _resident(layer, (1, d)), tile(d)],
        out_specs=tile(d),
        out_shape=jax.ShapeDtypeStruct((t, d), F32),
        compiler_params=_params(("arbitrary",)),
        name="mix_out",
    )(mix_c, mix_a, w_out, g_post, h)


def _bias_vectors(table):
    depth, n_rel, heads = table.shape
    clip = (n_rel - 1) // 2
    far = table[:, 2 * clip:2 * clip + 1, :]
    n_far = ATT_WINDOW - clip + 1
    ramp = table[:, 2 * clip - 1:0:-1, :]
    tail = BIAS_LANES - n_far - ramp.shape[1]
    t = jnp.concatenate([jnp.broadcast_to(far, (depth, n_far, heads)), ramp,
                         jnp.broadcast_to(far, (depth, tail, heads))], axis=1)
    return jnp.swapaxes(t, 1, 2).astype(F32)


def kernel(x_prompt, x_sample, cache_k, cache_v, state_conv, ln_ffn1_pre, ffn1_w_gate, ffn1_w_up,
           ffn1_w_down, ln_ffn1_post, ln_mix_pre, w_in, conv_w, rel_bias, g_conv_out, g_attn_out, w_out,
           ln_mix_post, ln_ffn2_pre, ffn2_w_gate, ffn2_w_up, ffn2_w_down, ln_ffn2_post):
    batch, seq, d = x_prompt.shape
    dec_batch, dec_seq, _ = x_sample.shape
    depth = w_in.shape[0]
    n_cache, n_heads, head_dim = cache_k.shape[2:]
    d_attn = n_heads * head_dim
    d_conv = conv_w.shape[2]
    keep = min(ATT_WINDOW, seq)

    tf = 2 * MXU_DIM
    tm_s = dec_batch * dec_seq
    assert n_heads % 2 == 0
    tiles = dict(prompt=dict(ffn=1024, proj=256, mix=1024), sample=dict(ffn=tm_s, proj=tm_s, mix=tm_s))

    gain = lambda a: a.reshape(depth, 1, -1)
    ffn1 = (gain(ln_ffn1_pre), ffn1_w_gate.astype(BF16), ffn1_w_up.astype(BF16), ffn1_w_down.astype(BF16),
            gain(ln_ffn1_post))
    ffn2 = (gain(ln_ffn2_pre), ffn2_w_gate.astype(BF16), ffn2_w_up.astype(BF16), ffn2_w_down.astype(BF16),
            gain(ln_ffn2_post))
    w_in_b = w_in.astype(BF16)
    w_out_b = w_out.astype(BF16)
    tvec = _bias_vectors(rel_bias)
    tvec_rep = jnp.repeat(tvec, n_heads, axis=-1)
    cache_kf = cache_k.reshape(depth, dec_batch, n_cache * n_heads, head_dim)
    cache_vf = cache_v.reshape(depth, dec_batch, n_cache * n_heads, head_dim)
    zero_state = jnp.zeros((1, batch, 2, d_conv), F32)

    def layer(l, h, tile, seq_len, conv_init, init_layer, kv_keep, attend):
        h = _ffn(h, *ffn1, layer=l, tm=tile["ffn"], tf=tf)
        q, k, v, mix_c, conv_new, k_keep, v_keep = _proj(
            h, gain(ln_mix_pre), w_in_b, conv_w, conv_init, gain(g_conv_out), kv_keep,
            layer=l, init_layer=init_layer, tm=tile["proj"], seq_len=seq_len, d_attn=d_attn)
        mix_a = attend(q, k, v)
        h = _mix_out(mix_c, mix_a, w_out_b, gain(ln_mix_post), h, layer=l, tm=tile["mix"])
        h = _ffn(h, *ffn2, layer=l, tm=tile["ffn"], tf=tf)
        return h, conv_new, (k_keep, v_keep)

    hp = x_prompt.reshape(batch * seq, d)
    hs = x_sample.reshape(dec_batch * dec_seq, d)
    kv_p = kv_s = None
    conv_p, conv_s = [], []
    for l in range(depth):
        attend_prompt = functools.partial(
            _attn_prompt, tvec=tvec, g_attn=gain(g_attn_out), layer=l, n_seq=batch, seq_len=seq,
            n_heads=n_heads, head_dim=head_dim)
        attend_sample = functools.partial(
            _attn_sample, cache_k=cache_kf, cache_v=cache_vf, tvec=tvec, tvec_rep=tvec_rep,
            g_attn=gain(g_attn_out), layer=l,
            n_seq=dec_batch, n_new=dec_seq, n_heads=n_heads, head_dim=head_dim)
        hp, c_p, kv_p = layer(l, hp, tiles["prompt"], seq, zero_state, 0, kv_p, attend_prompt)
        hs, c_s, kv_s = layer(l, hs, tiles["sample"], dec_seq, state_conv, l, kv_s, attend_sample)
        conv_p.append(c_p)
        conv_s.append(c_s)

    return (hp.reshape(batch, seq, d), hs.reshape(dec_batch, dec_seq, d),
            kv_p[0].reshape(depth, batch, keep, n_heads, head_dim),
            kv_p[1].reshape(depth, batch, keep, n_heads, head_dim),
            jnp.stack(conv_p),
            kv_s[0].reshape(depth, dec_batch, dec_seq, n_heads, head_dim),
            kv_s[1].reshape(depth, dec_batch, dec_seq, n_heads, head_dim),
            jnp.stack(conv_s))
```

```python
import functools
from typing import NamedTuple

import jax
import jax.numpy as jnp
from jax import lax
from jax.experimental import pallas as pl
from jax.experimental.pallas import tpu as pltpu

F32 = jnp.float32
BF16 = jnp.bfloat16

CHUNK = 64
LEFT_CHUNKS = 8
ATT_WINDOW = LEFT_CHUNKS * CHUNK
EPS = 1e-6
NEG_INF = -1e30

MXU_DIM = 256
V7X_VMEM_BYTES = 64 * 1024 * 1024
VMEM_LIMIT_BYTES = V7X_VMEM_BYTES - 4 * 1024 * 1024

PAIR = 2 * CHUNK
PAIR_KEYS = ATT_WINDOW + PAIR
BIAS_LANES = PAIR_KEYS + PAIR


def _rms(x, g):
    ms = jnp.mean(x * x, axis=-1, keepdims=True)
    return x * lax.rsqrt(ms + EPS) * g


def _params(semantics):
    return pltpu.CompilerParams(dimension_semantics=semantics, vmem_limit_bytes=VMEM_LIMIT_BYTES)


def _resident(layer, shape):
    return pl.BlockSpec((None,) + tuple(shape), lambda *_: (layer,) + (0,) * len(shape),
                        pipeline_mode=pl.Buffered(1))


def _ffn_kernel(x_ref, gpre_ref, wg_ref, wu_ref, wd_ref, gpost_ref, o_ref, u_ref, *, n_chunks, last_tf, rc):
    j = pl.program_id(1)
    tm = x_ref.shape[0]
    tf = wg_ref.shape[1]

    def gate_up(u, n):
        g = jnp.dot(u, wg_ref[:, :n], preferred_element_type=F32)
        up = jnp.dot(u, wu_ref[:, :n], preferred_element_type=F32)
        return (g * jax.nn.sigmoid(g) * up).astype(BF16)

    def down(h, n):
        return jnp.dot(h, wd_ref[:n, :], preferred_element_type=F32)

    @pl.when(j == 0)
    def _():
        for r in range(0, tm, rc):
            u = _rms(x_ref[r:r + rc, :], gpre_ref[...]).astype(BF16)
            u_ref[r:r + rc, :] = u
            o_ref[r:r + rc, :] = down(gate_up(u, tf), tf)

    @pl.when((j > 0) & (j < n_chunks - 1))
    def _():
        o_ref[...] += down(gate_up(u_ref[...], tf), tf)

    @pl.when(j == n_chunks - 1)
    def _():
        h = gate_up(u_ref[...], last_tf)
        for r in range(0, tm, rc):
            acc = o_ref[r:r + rc, :] + down(h[r:r + rc, :], last_tf)
            o_ref[r:r + rc, :] = x_ref[r:r + rc, :] + 0.5 * _rms(acc, gpost_ref[...])


def _ffn(x, g_pre, wg, wu, wd, g_post, *, layer, tm, tf):
    t, d = x.shape
    f = wg.shape[2]
    n_chunks = pl.cdiv(f, tf)
    last_tf = f - (n_chunks - 1) * tf
    assert t % tm == 0 and n_chunks >= 2
    rc = min(tm, MXU_DIM)
    gain = pl.BlockSpec((None, 1, d), lambda i, j: (layer, 0, 0))
    return pl.pallas_call(
        functools.partial(_ffn_kernel, n_chunks=n_chunks, last_tf=last_tf, rc=rc),
        grid=(t // tm, n_chunks),
        in_specs=[
            pl.BlockSpec((tm, d), lambda i, j: (i, 0)),
            gain,
            pl.BlockSpec((None, d, tf), lambda i, j: (layer, 0, j)),
            pl.BlockSpec((None, d, tf), lambda i, j: (layer, 0, j)),
            pl.BlockSpec((None, tf, d), lambda i, j: (layer, j, 0)),
            gain,
        ],
        out_specs=pl.BlockSpec((tm, d), lambda i, j: (i, 0)),
        out_shape=jax.ShapeDtypeStruct((t, d), F32),
        scratch_shapes=[pltpu.VMEM((tm, d), BF16)],
        compiler_params=_params(("arbitrary", "arbitrary")),
        name="ffn",
    )(x, g_pre, wg, wu, wd, g_post)


def _proj_kernel(h_ref, gpre_ref, win_ref, convw_ref, cinit_ref, gc_ref, *rest,
                 tm, seq_len, d_conv, d_attn, tail_tiles):
    q_ref, k_ref, v_ref, mixc_ref, cst_ref, kt_ref, vt_ref, carry_ref = rest[-8:]
    u = _rms(h_ref[...], gpre_ref[...]).astype(BF16)

    def proj(c0, n):
        return jnp.dot(u, win_ref[:, c0:c0 + n], preferred_element_type=F32)

    bg = proj(0, d_conv)
    gx = proj(d_conv, d_conv) * proj(2 * d_conv, d_conv)

    g1 = pltpu.roll(gx, 1, axis=0)
    g2 = pltpu.roll(gx, 2, axis=0)
    row = lax.broadcasted_iota(jnp.int32, (tm, 1), 0)
    if seq_len >= tm:
        ti = pl.program_id(0) % (seq_len // tm)
        first = ti == 0
        halo0 = jnp.where(first, cinit_ref[0, 0:1, :], carry_ref[0:1, :])
        halo1 = jnp.where(first, cinit_ref[0, 1:2, :], carry_ref[1:2, :])
        pos = row
        carry_ref[...] = gx[tm - 2:tm, :]
        cst_ref[0] = gx[tm - 2:tm, :]
    else:
        n_seq = tm // seq_len
        halo0 = jnp.broadcast_to(cinit_ref[:, 0:1, :], (n_seq, seq_len, d_conv)).reshape(tm, d_conv)
        halo1 = jnp.broadcast_to(cinit_ref[:, 1:2, :], (n_seq, seq_len, d_conv)).reshape(tm, d_conv)
        pos = row % seq_len
        cst_ref[...] = gx.reshape(n_seq, seq_len, d_conv)[:, seq_len - 2:seq_len, :]
    g1 = jnp.where(pos == 0, halo1, g1)
    g2 = jnp.where(pos == 0, halo0, jnp.where(pos == 1, halo1, g2))
    w = convw_ref[...]
    y = bg * (g2 * w[0:1, :] + g1 * w[1:2, :] + gx * w[2:3, :])
    mixc_ref[...] = _rms(y, gc_ref[...]).astype(BF16)

    q_ref[...] = proj(3 * d_conv, d_attn).astype(BF16)
    k = proj(3 * d_conv + d_attn, d_attn)
    v = proj(3 * d_conv + 2 * d_attn, d_attn)
    k_ref[...] = k.astype(BF16)
    v_ref[...] = v.astype(BF16)

    if seq_len >= tm:
        @pl.when(ti >= seq_len // tm - tail_tiles)
        def _():
            kt_ref[...] = k
            vt_ref[...] = v
    else:
        kt_ref[...] = k
        vt_ref[...] = v


def _proj(h, g_pre, w_in, conv_w, conv_init, g_conv, kv_keep, *, layer, init_layer, tm, seq_len, d_attn):
    t, d = h.shape
    depth, _, d_conv = conv_w.shape
    n_seq = t // seq_len
    keep = min(ATT_WINDOW, seq_len)
    if seq_len >= tm:
        assert seq_len % tm == 0 and keep % tm == 0
        tiles_per_seq = seq_len // tm
        tail_tiles = keep // tm
        seq_of = lambda i: i // tiles_per_seq
        state_block = (1, 2, d_conv)
        state_map = lambda i: (seq_of(i), 0, 0)
        tail_of = lambda i: (seq_of(i) * tail_tiles
                             + jnp.maximum(i % tiles_per_seq - (tiles_per_seq - tail_tiles), 0))
    else:
        assert tm % seq_len == 0 and keep == seq_len
        tail_tiles = 1
        state_block = (tm // seq_len, 2, d_conv)
        state_map = lambda i: (i, 0, 0)
        tail_of = lambda i: i
    tail_blocks = n_seq * keep // tm
    tail_spec = pl.BlockSpec((tm, d_attn), lambda i: (layer * tail_blocks + tail_of(i), 0))
    tile = lambda n: pl.BlockSpec((tm, n), lambda i: (i, 0))
    kern = functools.partial(_proj_kernel, tm=tm, seq_len=seq_len, d_conv=d_conv, d_attn=d_attn,
                             tail_tiles=tail_tiles)
    in_specs = [
        tile(d),
        _resident(layer, (1, d)),
        _resident(layer, w_in.shape[1:]),
        _resident(layer, conv_w.shape[1:]),
        pl.BlockSpec((None,) + state_block, lambda i: (init_layer,) + state_map(i)),
        _resident(layer, (1, d_conv)),
    ]
    args = [h, g_pre, w_in, conv_w, conv_init, g_conv]
    aliases = {}
    if kv_keep is not None:
        aliases = {len(args): 5, len(args) + 1: 6}
        in_specs += [pl.BlockSpec(memory_space=pl.ANY)] * 2
        args += list(kv_keep)
    cache_shape = jax.ShapeDtypeStruct((depth * n_seq * keep, d_attn), F32)
    return pl.pallas_call(
        kern,
        grid=(t // tm,),
        in_specs=in_specs,
        out_specs=[
            tile(d_attn), tile(d_attn), tile(d_attn), tile(d_conv),
            pl.BlockSpec(state_block, state_map),
            tail_spec, tail_spec,
        ],
        out_shape=[
            jax.ShapeDtypeStruct((t, d_attn), BF16),
            jax.ShapeDtypeStruct((t, d_attn), BF16),
            jax.ShapeDtypeStruct((t, d_attn), BF16),
            jax.ShapeDtypeStruct((t, d_conv), BF16),
            jax.ShapeDtypeStruct((n_seq, 2, d_conv), F32),
            cache_shape, cache_shape,
        ],
        input_output_aliases=aliases,
        scratch_shapes=[pltpu.VMEM((2, d_conv), F32)],
        compiler_params=_params(("arbitrary",)),
        name="proj",
    )(*args)


def _toeplitz_bias(tvec_ref, h, rows):
    t = jnp.broadcast_to(tvec_ref[h:h + 1, :], (rows, BIAS_LANES))
    return pltpu.roll(t, 0, axis=1, stride=1, stride_axis=0)


def _attn_prompt_kernel(q_ref, kp_ref, kc_ref, vp_ref, vc_ref, tvec_ref, ga_ref, o_ref,
                        bias_ref, y_ref, *, n_heads, head_dim, blk):
    qb = pl.program_id(1)

    @pl.when((pl.program_id(0) == 0) & (qb == 0))
    def _():
        qi = lax.broadcasted_iota(jnp.int32, (PAIR, PAIR_KEYS), 0)
        kk = lax.broadcasted_iota(jnp.int32, (PAIR, PAIR_KEYS), 1)
        lo = (qi // CHUNK) * CHUNK
        band = (kk >= lo) & (kk < lo + ATT_WINDOW + CHUNK)
        for h in range(n_heads):
            bias_ref[h] = jnp.where(band, _toeplitz_bias(tvec_ref, h, PAIR)[:, :PAIR_KEYS], NEG_INF)

    scale = head_dim ** -0.5
    upper = lax.broadcasted_iota(jnp.int32, (PAIR, 2 * head_dim), 1) < head_dim

    def attend(first_block):
        for r0 in range(0, blk, PAIR):
            k0 = blk - r0 if first_block else 0
            for h in range(0, n_heads, 2):
                c0 = h * head_dim
                cols = slice(c0, c0 + 2 * head_dim)
                q2 = q_ref[r0:r0 + PAIR, cols]
                if first_block:
                    k2 = kc_ref[0:r0 + PAIR, cols]
                    v2 = vc_ref[0:r0 + PAIR, cols]
                else:
                    k2 = jnp.concatenate([kp_ref[r0:blk, cols], kc_ref[0:r0 + PAIR, cols]], axis=0)
                    v2 = jnp.concatenate([vp_ref[r0:blk, cols], vc_ref[0:r0 + PAIR, cols]], axis=0)
                zero = jnp.zeros_like(q2)
                qbd = jnp.concatenate([jnp.where(upper, q2, zero), jnp.where(upper, zero, q2)], axis=0)
                s = lax.dot_general(qbd, k2, (((1,), (1,)), ((), ())), preferred_element_type=F32)
                bias = bias_ref[h:h + 2, :, k0:].reshape(2 * PAIR, PAIR_KEYS - k0)
                s = s * scale + bias
                e = jnp.exp(s - jnp.max(s, axis=-1, keepdims=True))
                l = jnp.sum(e, axis=-1, keepdims=True)
                o = jnp.dot(e.astype(BF16), v2, preferred_element_type=F32)
                y_ref[r0:r0 + PAIR, c0:c0 + head_dim] = o[:PAIR, :head_dim] / l[:PAIR]
                y_ref[r0:r0 + PAIR, c0 + head_dim:c0 + 2 * head_dim] = o[PAIR:, head_dim:] / l[PAIR:]

    pl.when(qb == 0)(functools.partial(attend, True))
    pl.when(qb != 0)(functools.partial(attend, False))
    o_ref[...] = _rms(y_ref[...], ga_ref[...]).astype(BF16)


def _attn_prompt(q, k, v, tvec, g_attn, *, layer, n_seq, seq_len, n_heads, head_dim):
    t, d_attn = q.shape
    blk = ATT_WINDOW
    assert seq_len % blk == 0 and blk % PAIR == 0
    nb = seq_len // blk
    cur = pl.BlockSpec((blk, d_attn), lambda b, i: (b * nb + i, 0))
    prev = pl.BlockSpec((blk, d_attn), lambda b, i: (b * nb + jnp.maximum(i - 1, 0), 0))
    kern = functools.partial(_attn_prompt_kernel, n_heads=n_heads, head_dim=head_dim, blk=blk)
    return pl.pallas_call(
        kern,
        grid=(n_seq, nb),
        in_specs=[cur, prev, cur, prev, cur,
                  _resident(layer, tvec.shape[1:]), _resident(layer, (1, d_attn))],
        out_specs=cur,
        out_shape=jax.ShapeDtypeStruct((t, d_attn), BF16),
        scratch_shapes=[
            pltpu.VMEM((n_heads, PAIR, PAIR_KEYS), F32),
            pltpu.VMEM((blk, d_attn), F32),
        ],
        compiler_params=_params(("arbitrary", "arbitrary")),
        name="attn_prompt",
    )(q, k, k, v, v, tvec, g_attn)


def _attn_sample_kernel(q_ref, kn_ref, vn_ref, kc_ref, vc_ref, tvec_ref, tvec_rep_ref, ga_ref, o_ref,
                        bias_ref, y_ref, *, n_heads, head_dim, n_new, n_cache):
    cols = n_cache * n_heads

    @pl.when(pl.program_id(0) == 0)
    def _():
        col_head = lax.broadcasted_iota(jnp.int32, (n_new, cols), 1) % n_heads
        for h in range(n_heads):
            t = jnp.broadcast_to(tvec_rep_ref[h:h + 1, :], (n_new, tvec_rep_ref.shape[1]))
            shifted = pltpu.roll(t, 0, axis=1, stride=n_heads, stride_axis=0)[:, :cols]
            bias_ref[h * n_new:(h + 1) * n_new, :] = jnp.where(col_head == h, shifted, NEG_INF)

    scale = head_dim ** -0.5
    nt = (((1,), (1,)), ((), ()))
    head = lambda ref, h: ref[:, h * head_dim:(h + 1) * head_dim]
    q_all = jnp.concatenate([head(q_ref, h) for h in range(n_heads)], axis=0)
    sc = (lax.dot_general(q_all, kc_ref[...].astype(BF16), nt, preferred_element_type=F32) * scale
          + bias_ref[...])
    sn = jnp.concatenate(
        [lax.dot_general(head(q_ref, h), head(kn_ref, h), nt, preferred_element_type=F32) * scale
         + _toeplitz_bias(tvec_ref, h, n_new)[:, n_cache:n_cache + n_new] for h in range(n_heads)], axis=0)
    m = jnp.maximum(jnp.max(sc, axis=-1, keepdims=True), jnp.max(sn, axis=-1, keepdims=True))
    ec = jnp.exp(sc - m)
    en = jnp.exp(sn - m)
    l = jnp.sum(ec, axis=-1, keepdims=True) + jnp.sum(en, axis=-1, keepdims=True)
    o = jnp.dot(ec.astype(BF16), vc_ref[...].astype(BF16), preferred_element_type=F32)
    o = o + jnp.concatenate(
        [jnp.dot(en[h * n_new:(h + 1) * n_new, :].astype(BF16), head(vn_ref, h), preferred_element_type=F32)
         for h in range(n_heads)], axis=0)
    o = o / l
    for h in range(n_heads):
        y_ref[:, h * head_dim:(h + 1) * head_dim] = o[h * n_new:(h + 1) * n_new, :]
    o_ref[...] = _rms(y_ref[...], ga_ref[...]).astype(BF16)


def _attn_sample(q, k, v, cache_k, cache_v, tvec, tvec_rep, g_attn, *, layer, n_seq, n_new, n_heads, head_dim):
    t, d_attn = q.shape
    n_cache = cache_k.shape[2] // n_heads
    assert n_cache == ATT_WINDOW and n_new <= PAIR
    new = pl.BlockSpec((n_new, d_attn), lambda b: (b, 0))
    cache = pl.BlockSpec((None, None, n_cache * n_heads, head_dim), lambda b: (layer, b, 0, 0))
    kern = functools.partial(_attn_sample_kernel, n_heads=n_heads, head_dim=head_dim,
                             n_new=n_new, n_cache=n_cache)
    return pl.pallas_call(
        kern,
        grid=(n_seq,),
        in_specs=[new, new, new, cache, cache, _resident(layer, tvec.shape[1:]),
                  _resident(layer, tvec_rep.shape[1:]), _resident(layer, (1, d_attn))],
        out_specs=new,
        out_shape=jax.ShapeDtypeStruct((t, d_attn), BF16),
        scratch_shapes=[pltpu.VMEM((n_heads * n_new, n_cache * n_heads), F32),
                        pltpu.VMEM((n_new, d_attn), F32)],
        compiler_params=_params(("arbitrary",)),
        name="attn_sample",
    )(q, k, v, cache_k, cache_v, tvec, tvec_rep, g_attn)


def _mix_out_kernel(mc_ref, ma_ref, wo_ref, gpost_ref, h_ref, o_ref, *, d_conv, rc):
    for r in range(0, h_ref.shape[0], rc):
        z = (jnp.dot(mc_ref[r:r + rc, :], wo_ref[0:d_conv, :], preferred_element_type=F32)
             + jnp.dot(ma_ref[r:r + rc, :], wo_ref[d_conv:, :], preferred_element_type=F32))
        o_ref[r:r + rc, :] = h_ref[r:r + rc, :] + _rms(z, gpost_ref[...])


def _mix_out(mix_c, mix_a, w_out, g_post, h, *, layer, tm):
    t, d = h.shape
    d_conv = mix_c.shape[1]
    d_attn = mix_a.shape[1]
    tile = lambda n: pl.BlockSpec((tm, n), lambda i: (i, 0))
    return pl.pallas_call(
        functools.partial(_mix_out_kernel, d_conv=d_conv, rc=min(tm, MXU_DIM)),
        grid=(t // tm,),
        in_specs=[tile(d_conv), tile(d_attn), _resident(layer, w_out.shape[1:]),
                  _resident(layer, (1, d)), tile(d)],
        out_specs=tile(d),
        out_shape=jax.ShapeDtypeStruct((t, d), F32),
        compiler_params=_params(("arbitrary",)),
        name="mix_out",
    )(mix_c, mix_a, w_out, g_post, h)


class _Tiles(NamedTuple):
    ffn: int
    proj: int
    mix: int


def _tile_plan(n_tokens, seq_len):
    fit = lambda rows: rows if n_tokens % rows == 0 and n_tokens >= rows else n_tokens
    proj = fit(MXU_DIM)
    if seq_len >= proj:
        assert seq_len % proj == 0 and min(ATT_WINDOW, seq_len) % proj == 0
    return _Tiles(ffn=fit(4 * MXU_DIM), proj=proj, mix=fit(4 * MXU_DIM))


def _bias_vectors(table):
    depth, n_rel, heads = table.shape
    clip = (n_rel - 1) // 2
    far = table[:, 2 * clip:2 * clip + 1, :]
    n_far = ATT_WINDOW - clip + 1
    ramp = table[:, 2 * clip - 1:0:-1, :]
    tail = BIAS_LANES - n_far - ramp.shape[1]
    t = jnp.concatenate([jnp.broadcast_to(far, (depth, n_far, heads)), ramp,
                         jnp.broadcast_to(far, (depth, tail, heads))], axis=1)
    return jnp.swapaxes(t, 1, 2).astype(F32)


def kernel(x_prompt, x_sample, cache_k, cache_v, state_conv, ln_ffn1_pre, ffn1_w_gate, ffn1_w_up,
           ffn1_w_down, ln_ffn1_post, ln_mix_pre, w_in, conv_w, rel_bias, g_conv_out, g_attn_out, w_out,
           ln_mix_post, ln_ffn2_pre, ffn2_w_gate, ffn2_w_up, ffn2_w_down, ln_ffn2_post):
    batch, seq, d = x_prompt.shape
    dec_batch, dec_seq, _ = x_sample.shape
    depth = w_in.shape[0]
    n_cache, n_heads, head_dim = cache_k.shape[2:]
    d_attn = n_heads * head_dim
    d_conv = conv_w.shape[2]
    keep = min(ATT_WINDOW, seq)

    tf = 2 * MXU_DIM
    assert n_heads % 2 == 0
    tiles_p = _tile_plan(batch * seq, seq)
    tiles_s = _tile_plan(dec_batch * dec_seq, dec_seq)

    gain = lambda a: a.reshape(depth, 1, -1)
    ffn1 = (gain(ln_ffn1_pre), ffn1_w_gate.astype(BF16), ffn1_w_up.astype(BF16), ffn1_w_down.astype(BF16),
            gain(ln_ffn1_post))
    ffn2 = (gain(ln_ffn2_pre), ffn2_w_gate.astype(BF16), ffn2_w_up.astype(BF16), ffn2_w_down.astype(BF16),
            gain(ln_ffn2_post))
    w_in_b = w_in.astype(BF16)
    w_out_b = w_out.astype(BF16)
    tvec = _bias_vectors(rel_bias)
    tvec_rep = jnp.repeat(tvec, n_heads, axis=-1)
    cache_kf = cache_k.reshape(depth, dec_batch, n_cache * n_heads, head_dim)
    cache_vf = cache_v.reshape(depth, dec_batch, n_cache * n_heads, head_dim)
    zero_state = jnp.zeros((1, batch, 2, d_conv), F32)

    def layer(l, h, tile, seq_len, conv_init, init_layer, kv_keep, attend):
        h = _ffn(h, *ffn1, layer=l, tm=tile.ffn, tf=tf)
        q, k, v, mix_c, conv_new, k_keep, v_keep = _proj(
            h, gain(ln_mix_pre), w_in_b, conv_w, conv_init, gain(g_conv_out), kv_keep,
            layer=l, init_layer=init_layer, tm=tile.proj, seq_len=seq_len, d_attn=d_attn)
        mix_a = attend(q, k, v)
        h = _mix_out(mix_c, mix_a, w_out_b, gain(ln_mix_post), h, layer=l, tm=tile.mix)
        h = _ffn(h, *ffn2, layer=l, tm=tile.ffn, tf=tf)
        return h, conv_new, (k_keep, v_keep)

    hp = x_prompt.reshape(batch * seq, d)
    hs = x_sample.reshape(dec_batch * dec_seq, d)
    kv_p = kv_s = None
    conv_p, conv_s = [], []
    for l in range(depth):
        attend_prompt = functools.partial(
            _attn_prompt, tvec=tvec, g_attn=gain(g_attn_out), layer=l, n_seq=batch, seq_len=seq,
            n_heads=n_heads, head_dim=head_dim)
        attend_sample = functools.partial(
            _attn_sample, cache_k=cache_kf, cache_v=cache_vf, tvec=tvec, tvec_rep=tvec_rep,
            g_attn=gain(g_attn_out), layer=l,
            n_seq=dec_batch, n_new=dec_seq, n_heads=n_heads, head_dim=head_dim)
        hp, c_p, kv_p = layer(l, hp, tiles_p, seq, zero_state, 0, kv_p, attend_prompt)
        hs, c_s, kv_s = layer(l, hs, tiles_s, dec_seq, state_conv, l, kv_s, attend_sample)
        conv_p.append(c_p)
        conv_s.append(c_s)

    return (hp.reshape(batch, seq, d), hs.reshape(dec_batch, dec_seq, d),
            kv_p[0].reshape(depth, batch, keep, n_heads, head_dim),
            kv_p[1].reshape(depth, batch, keep, n_heads, head_dim),
            jnp.stack(conv_p),
            kv_s[0].reshape(depth, dec_batch, dec_seq, n_heads, head_dim),
            kv_s[1].reshape(depth, dec_batch, dec_seq, n_heads, head_dim),
            jnp.stack(conv_s))
```

```python
import functools
from typing import NamedTuple

import jax
import jax.numpy as jnp
from jax import lax
from jax.experimental import pallas as pl
from jax.experimental.pallas import tpu as pltpu

F32 = jnp.float32
BF16 = jnp.bfloat16

CHUNK = 64
LEFT_CHUNKS = 8
ATT_WINDOW = LEFT_CHUNKS * CHUNK
EPS = 1e-6
NEG_INF = -1e30

MXU_DIM = 256
V7X_VMEM_BYTES = 64 * 1024 * 1024
VMEM_LIMIT_BYTES = V7X_VMEM_BYTES - 4 * 1024 * 1024

PAIR = 2 * CHUNK
PAIR_KEYS = ATT_WINDOW + PAIR
BIAS_LANES = PAIR_KEYS + PAIR


def _rms(x, g):
    ms = jnp.mean(x * x, axis=-1, keepdims=True)
    return x * lax.rsqrt(ms + EPS) * g


def _params(semantics):
    return pltpu.CompilerParams(dimension_semantics=semantics, vmem_limit_bytes=VMEM_LIMIT_BYTES)


def _resident(layer, shape):
    return pl.BlockSpec((None,) + tuple(shape), lambda *_: (layer,) + (0,) * len(shape),
                        pipeline_mode=pl.Buffered(1))


def _ffn_kernel(x_ref, gpre_ref, wg_ref, wu_ref, wd_ref, gpost_ref, o_ref, u_ref, *, n_chunks, last_tf, rc):
    j = pl.program_id(1)
    tm = x_ref.shape[0]
    tf = wg_ref.shape[1]
    first, last = j == 0, j == n_chunks - 1
    middle = (j > 0) & (j < n_chunks - 1)

    def gate_up(u, n):
        g = jnp.dot(u, wg_ref[:, :n], preferred_element_type=F32)
        up = jnp.dot(u, wu_ref[:, :n], preferred_element_type=F32)
        return (g * jax.nn.sigmoid(g) * up).astype(BF16)

    def down(h, n):
        return jnp.dot(h, wd_ref[:n, :], preferred_element_type=F32)

    @pl.when(first)
    def _():
        for r in range(0, tm, rc):
            u = _rms(x_ref[r:r + rc, :], gpre_ref[...]).astype(BF16)
            u_ref[r:r + rc, :] = u
            o_ref[r:r + rc, :] = down(gate_up(u, tf), tf)

    @pl.when(middle)
    def _():
        o_ref[...] += down(gate_up(u_ref[...], tf), tf)

    @pl.when(last)
    def _():
        h = gate_up(u_ref[...], last_tf)
        for r in range(0, tm, rc):
            acc = o_ref[r:r + rc, :] + down(h[r:r + rc, :], last_tf)
            o_ref[r:r + rc, :] = x_ref[r:r + rc, :] + 0.5 * _rms(acc, gpost_ref[...])


def _ffn(x, g_pre, wg, wu, wd, g_post, *, layer, tm, tf):
    t, d = x.shape
    f = wg.shape[2]
    n_chunks = pl.cdiv(f, tf)
    last_tf = f - (n_chunks - 1) * tf
    assert t % tm == 0 and n_chunks >= 2
    rc = min(tm, MXU_DIM)
    gain = pl.BlockSpec((None, 1, d), lambda i, j: (layer, 0, 0))
    return pl.pallas_call(
        functools.partial(_ffn_kernel, n_chunks=n_chunks, last_tf=last_tf, rc=rc),
        grid=(t // tm, n_chunks),
        in_specs=[
            pl.BlockSpec((tm, d), lambda i, j: (i, 0)),
            gain,
            pl.BlockSpec((None, d, tf), lambda i, j: (layer, 0, j)),
            pl.BlockSpec((None, d, tf), lambda i, j: (layer, 0, j)),
            pl.BlockSpec((None, tf, d), lambda i, j: (layer, j, 0)),
            gain,
        ],
        out_specs=pl.BlockSpec((tm, d), lambda i, j: (i, 0)),
        out_shape=jax.ShapeDtypeStruct((t, d), F32),
        scratch_shapes=[pltpu.VMEM((tm, d), BF16)],
        compiler_params=_params(("arbitrary", "arbitrary")),
        name="ffn",
    )(x, g_pre, wg, wu, wd, g_post)


def _proj_kernel(h_ref, gpre_ref, win_ref, convw_ref, cinit_ref, gc_ref, *rest,
                 tm, seq_len, d_conv, d_attn, tail_tiles, rc):
    q_ref, k_ref, v_ref, mixc_ref, cst_ref, kt_ref, vt_ref, carry_ref = rest[-8:]
    bg_parts, gx_parts, k_parts, v_parts = [], [], [], []
    for r in range(0, tm, rc):
        u = _rms(h_ref[r:r + rc, :], gpre_ref[...]).astype(BF16)

        def proj(c0, n):
            return jnp.dot(u, win_ref[:, c0:c0 + n], preferred_element_type=F32)

        bg_parts.append(proj(0, d_conv))
        gx_parts.append(proj(d_conv, d_conv) * proj(2 * d_conv, d_conv))
        q_ref[r:r + rc, :] = proj(3 * d_conv, d_attn).astype(BF16)
        k_parts.append(proj(3 * d_conv + d_attn, d_attn))
        v_parts.append(proj(3 * d_conv + 2 * d_attn, d_attn))
    bg = jnp.concatenate(bg_parts, axis=0)
    gx = jnp.concatenate(gx_parts, axis=0)
    k = jnp.concatenate(k_parts, axis=0)
    v = jnp.concatenate(v_parts, axis=0)

    g1 = pltpu.roll(gx, 1, axis=0)
    g2 = pltpu.roll(gx, 2, axis=0)
    row = lax.broadcasted_iota(jnp.int32, (tm, 1), 0)
    if seq_len >= tm:
        ti = pl.program_id(0) % (seq_len // tm)
        first = ti == 0
        halo0 = jnp.where(first, cinit_ref[0, 0:1, :], carry_ref[0:1, :])
        halo1 = jnp.where(first, cinit_ref[0, 1:2, :], carry_ref[1:2, :])
        pos = row
        carry_ref[...] = gx[tm - 2:tm, :]
        cst_ref[0] = gx[tm - 2:tm, :]
    else:
        n_seq = tm // seq_len
        halo0 = jnp.broadcast_to(cinit_ref[:, 0:1, :], (n_seq, seq_len, d_conv)).reshape(tm, d_conv)
        halo1 = jnp.broadcast_to(cinit_ref[:, 1:2, :], (n_seq, seq_len, d_conv)).reshape(tm, d_conv)
        pos = row % seq_len
        cst_ref[...] = gx.reshape(n_seq, seq_len, d_conv)[:, seq_len - 2:seq_len, :]
    g1 = jnp.where(pos == 0, halo1, g1)
    g2 = jnp.where(pos == 0, halo0, jnp.where(pos == 1, halo1, g2))
    w = convw_ref[...]
    y = bg * (g2 * w[0:1, :] + g1 * w[1:2, :] + gx * w[2:3, :])
    mixc_ref[...] = _rms(y, gc_ref[...]).astype(BF16)

    k_ref[...] = k.astype(BF16)
    v_ref[...] = v.astype(BF16)

    if seq_len >= tm:
        @pl.when(ti >= seq_len // tm - tail_tiles)
        def _():
            kt_ref[...] = k
            vt_ref[...] = v
    else:
        kt_ref[...] = k
        vt_ref[...] = v


def _proj(h, g_pre, w_in, conv_w, conv_init, g_conv, kv_keep, *, layer, init_layer, tm, seq_len, d_attn):
    t, d = h.shape
    depth, _, d_conv = conv_w.shape
    n_seq = t // seq_len
    keep = min(ATT_WINDOW, seq_len)
    if seq_len >= tm:
        assert seq_len % tm == 0 and keep % tm == 0
        tiles_per_seq = seq_len // tm
        tail_tiles = keep // tm
        seq_of = lambda i: i // tiles_per_seq
        state_block = (1, 2, d_conv)
        state_map = lambda i: (seq_of(i), 0, 0)
        tail_of = lambda i: (seq_of(i) * tail_tiles
                             + jnp.maximum(i % tiles_per_seq - (tiles_per_seq - tail_tiles), 0))
    else:
        assert tm % seq_len == 0 and keep == seq_len
        tail_tiles = 1
        state_block = (tm // seq_len, 2, d_conv)
        state_map = lambda i: (i, 0, 0)
        tail_of = lambda i: i
    tail_blocks = n_seq * keep // tm
    tail_spec = pl.BlockSpec((tm, d_attn), lambda i: (layer * tail_blocks + tail_of(i), 0))
    tile = lambda n: pl.BlockSpec((tm, n), lambda i: (i, 0))
    kern = functools.partial(_proj_kernel, tm=tm, seq_len=seq_len, d_conv=d_conv, d_attn=d_attn,
                             tail_tiles=tail_tiles, rc=min(tm, MXU_DIM))
    in_specs = [
        tile(d),
        _resident(layer, (1, d)),
        _resident(layer, w_in.shape[1:]),
        _resident(layer, conv_w.shape[1:]),
        pl.BlockSpec((None,) + state_block, lambda i: (init_layer,) + state_map(i)),
        _resident(layer, (1, d_conv)),
    ]
    args = [h, g_pre, w_in, conv_w, conv_init, g_conv]
    aliases = {}
    if kv_keep is not None:
        aliases = {len(args): 5, len(args) + 1: 6}
        in_specs += [pl.BlockSpec(memory_space=pl.ANY)] * 2
        args += list(kv_keep)
    cache_shape = jax.ShapeDtypeStruct((depth * n_seq * keep, d_attn), F32)
    return pl.pallas_call(
        kern,
        grid=(t // tm,),
        in_specs=in_specs,
        out_specs=[
            tile(d_attn), tile(d_attn), tile(d_attn), tile(d_conv),
            pl.BlockSpec(state_block, state_map),
            tail_spec, tail_spec,
        ],
        out_shape=[
            jax.ShapeDtypeStruct((t, d_attn), BF16),
            jax.ShapeDtypeStruct((t, d_attn), BF16),
            jax.ShapeDtypeStruct((t, d_attn), BF16),
            jax.ShapeDtypeStruct((t, d_conv), BF16),
            jax.ShapeDtypeStruct((n_seq, 2, d_conv), F32),
            cache_shape, cache_shape,
        ],
        input_output_aliases=aliases,
        scratch_shapes=[pltpu.VMEM((2, d_conv), F32)],
        compiler_params=_params(("arbitrary",)),
        name="proj",
    )(*args)


def _toeplitz_bias(tvec_ref, h, rows):
    t = jnp.broadcast_to(tvec_ref[h:h + 1, :], (rows, BIAS_LANES))
    return pltpu.roll(t, 0, axis=1, stride=1, stride_axis=0)


def _attn_prompt_kernel(q_ref, kp_ref, kc_ref, vp_ref, vc_ref, tvec_ref, ga_ref, o_ref,
                        bias_ref, y_ref, *, n_heads, head_dim, blk):
    qb = pl.program_id(1)

    @pl.when((pl.program_id(0) == 0) & (qb == 0))
    def _():
        qi = lax.broadcasted_iota(jnp.int32, (PAIR, PAIR_KEYS), 0)
        kk = lax.broadcasted_iota(jnp.int32, (PAIR, PAIR_KEYS), 1)
        lo = (qi // CHUNK) * CHUNK
        band = (kk >= lo) & (kk < lo + ATT_WINDOW + CHUNK)
        for h in range(n_heads):
            bias_ref[h] = jnp.where(band, _toeplitz_bias(tvec_ref, h, PAIR)[:, :PAIR_KEYS], NEG_INF)

    scale = head_dim ** -0.5
    upper = lax.broadcasted_iota(jnp.int32, (PAIR, 2 * head_dim), 1) < head_dim

    def attend(first_block):
        for r0 in range(0, blk, PAIR):
            k0 = blk - r0 if first_block else 0
            for h in range(0, n_heads, 2):
                c0 = h * head_dim
                cols = slice(c0, c0 + 2 * head_dim)
                q2 = q_ref[r0:r0 + PAIR, cols]
                if first_block:
                    k2 = kc_ref[0:r0 + PAIR, cols]
                    v2 = vc_ref[0:r0 + PAIR, cols]
                else:
                    k2 = jnp.concatenate([kp_ref[r0:blk, cols], kc_ref[0:r0 + PAIR, cols]], axis=0)
                    v2 = jnp.concatenate([vp_ref[r0:blk, cols], vc_ref[0:r0 + PAIR, cols]], axis=0)
                zero = jnp.zeros_like(q2)
                qbd = jnp.concatenate([jnp.where(upper, q2, zero), jnp.where(upper, zero, q2)], axis=0)
                s = lax.dot_general(qbd, k2, (((1,), (1,)), ((), ())), preferred_element_type=F32)
                bias = bias_ref[h:h + 2, :, k0:].reshape(2 * PAIR, PAIR_KEYS - k0)
                s = s * scale + bias
                e = jnp.exp(s - jnp.max(s, axis=-1, keepdims=True))
                l = jnp.sum(e, axis=-1, keepdims=True)
                o = jnp.dot(e.astype(BF16), v2, preferred_element_type=F32)
                y_ref[r0:r0 + PAIR, c0:c0 + head_dim] = o[:PAIR, :head_dim] / l[:PAIR]
                y_ref[r0:r0 + PAIR, c0 + head_dim:c0 + 2 * head_dim] = o[PAIR:, head_dim:] / l[PAIR:]
        o_ref[...] = _rms(y_ref[...], ga_ref[...]).astype(BF16)

    pl.when(qb == 0)(functools.partial(attend, True))
    pl.when(qb != 0)(functools.partial(attend, False))


def _attn_prompt(q, k, v, tvec, g_attn, *, layer, n_seq, seq_len, n_heads, head_dim):
    t, d_attn = q.shape
    blk = ATT_WINDOW
    assert seq_len % blk == 0 and blk % PAIR == 0
    nb = seq_len // blk
    cur = pl.BlockSpec((blk, d_attn), lambda b, i: (b * nb + i, 0))
    prev = pl.BlockSpec((blk, d_attn), lambda b, i: (b * nb + jnp.maximum(i - 1, 0), 0))
    kern = functools.partial(_attn_prompt_kernel, n_heads=n_heads, head_dim=head_dim, blk=blk)
    return pl.pallas_call(
        kern,
        grid=(n_seq, nb),
        in_specs=[cur, prev, cur, prev, cur,
                  _resident(layer, tvec.shape[1:]), _resident(layer, (1, d_attn))],
        out_specs=cur,
        out_shape=jax.ShapeDtypeStruct((t, d_attn), BF16),
        scratch_shapes=[
            pltpu.VMEM((n_heads, PAIR, PAIR_KEYS), F32),
            pltpu.VMEM((blk, d_attn), F32),
        ],
        compiler_params=_params(("arbitrary", "arbitrary")),
        name="attn_prompt",
    )(q, k, k, v, v, tvec, g_attn)


def _attn_sample_kernel(q_ref, kn_ref, vn_ref, kc_ref, vc_ref, tvec_ref, tvec_rep_ref, ga_ref, o_ref,
                        bias_ref, y_ref, *, n_heads, head_dim, n_new, n_cache):
    cols = n_cache * n_heads

    @pl.when(pl.program_id(0) == 0)
    def _():
        col_head = lax.broadcasted_iota(jnp.int32, (n_new, cols), 1) % n_heads
        for h in range(n_heads):
            t = jnp.broadcast_to(tvec_rep_ref[h:h + 1, :], (n_new, tvec_rep_ref.shape[1]))
            shifted = pltpu.roll(t, 0, axis=1, stride=n_heads, stride_axis=0)[:, :cols]
            bias_ref[h * n_new:(h + 1) * n_new, :] = jnp.where(col_head == h, shifted, NEG_INF)

    scale = head_dim ** -0.5
    nt = (((1,), (1,)), ((), ()))
    head = lambda ref, h: ref[:, h * head_dim:(h + 1) * head_dim]
    q_all = jnp.concatenate([head(q_ref, h) for h in range(n_heads)], axis=0)
    sc = (lax.dot_general(q_all, kc_ref[...].astype(BF16), nt, preferred_element_type=F32) * scale
          + bias_ref[...])
    sn = jnp.concatenate(
        [lax.dot_general(head(q_ref, h), head(kn_ref, h), nt, preferred_element_type=F32) * scale
         + _toeplitz_bias(tvec_ref, h, n_new)[:, n_cache:n_cache + n_new] for h in range(n_heads)], axis=0)
    m = jnp.maximum(jnp.max(sc, axis=-1, keepdims=True), jnp.max(sn, axis=-1, keepdims=True))
    ec = jnp.exp(sc - m)
    en = jnp.exp(sn - m)
    l = jnp.sum(ec, axis=-1, keepdims=True) + jnp.sum(en, axis=-1, keepdims=True)
    o = jnp.dot(ec.astype(BF16), vc_ref[...].astype(BF16), preferred_element_type=F32)
    o = o + jnp.concatenate(
        [jnp.dot(en[h * n_new:(h + 1) * n_new, :].astype(BF16), head(vn_ref, h), preferred_element_type=F32)
         for h in range(n_heads)], axis=0)
    o = o / l
    for h in range(n_heads):
        y_ref[:, h * head_dim:(h + 1) * head_dim] = o[h * n_new:(h + 1) * n_new, :]
    o_ref[...] = _rms(y_ref[...], ga_ref[...]).astype(BF16)


def _attn_sample(q, k, v, cache_k, cache_v, tvec, tvec_rep, g_attn, *, layer, n_seq, n_new, n_heads, head_dim):
    t, d_attn = q.shape
    n_cache = cache_k.shape[2] // n_heads
    assert n_cache == ATT_WINDOW and n_new <= PAIR
    new = pl.BlockSpec((n_new, d_attn), lambda b: (b, 0))
    cache = pl.BlockSpec((None, None, n_cache * n_heads, head_dim), lambda b: (layer, b, 0, 0))
    kern = functools.partial(_attn_sample_kernel, n_heads=n_heads, head_dim=head_dim,
                             n_new=n_new, n_cache=n_cache)
    return pl.pallas_call(
        kern,
        grid=(n_seq,),
        in_specs=[new, new, new, cache, cache, _resident(layer, tvec.shape[1:]),
                  _resident(layer, tvec_rep.shape[1:]), _resident(layer, (1, d_attn))],
        out_specs=new,
        out_shape=jax.ShapeDtypeStruct((t, d_attn), BF16),
        scratch_shapes=[pltpu.VMEM((n_heads * n_new, n_cache * n_heads), F32),
                        pltpu.VMEM((n_new, d_attn), F32)],
        compiler_params=_params(("arbitrary",)),
        name="attn_sample",
    )(q, k, v, cache_k, cache_v, tvec, tvec_rep, g_attn)


def _mix_out_kernel(mc_ref, ma_ref, wo_ref, gpost_ref, h_ref, o_ref, *, d_conv, rc):
    for r in range(0, h_ref.shape[0], rc):
        z = (jnp.dot(mc_ref[r:r + rc, :], wo_ref[0:d_conv, :], preferred_element_type=F32)
             + jnp.dot(ma_ref[r:r + rc, :], wo_ref[d_conv:, :], preferred_element_type=F32))
        o_ref[r:r + rc, :] = h_ref[r:r + rc, :] + _rms(z, gpost_ref[...])


def _mix_out(mix_c, mix_a, w_out, g_post, h, *, layer, tm):
    t, d = h.shape
    d_conv = mix_c.shape[1]
    d_attn = mix_a.shape[1]
    tile = lambda n: pl.BlockSpec((tm, n), lambda i: (i, 0))
    return pl.pallas_call(
        functools.partial(_mix_out_kernel, d_conv=d_conv, rc=min(tm, MXU_DIM)),
        grid=(t // tm,),
        in_specs=[tile(d_conv), tile(d_attn), _resident(layer, w_out.shape[1:]),
                  _resident(layer, (1, d)), tile(d)],
        out_specs=tile(d),
        out_shape=jax.ShapeDtypeStruct((t, d), F32),
        compiler_params=_params(("arbitrary",)),
        name="mix_out",
    )(mix_c, mix_a, w_out, g_post, h)


class _Tiles(NamedTuple):
    ffn: int
    proj: int
    mix: int


def _tile_plan(n_tokens, seq_len):
    fit = lambda rows: rows if n_tokens % rows == 0 and n_tokens >= rows else n_tokens
    proj = fit(2 * MXU_DIM)
    if seq_len >= proj:
        assert seq_len % proj == 0 and min(ATT_WINDOW, seq_len) % proj == 0
    return _Tiles(ffn=fit(4 * MXU_DIM), proj=proj, mix=fit(4 * MXU_DIM))


def _bias_vectors(table):
    depth, n_rel, heads = table.shape
    clip = (n_rel - 1) // 2
    far = table[:, 2 * clip:2 * clip + 1, :]
    n_far = ATT_WINDOW - clip + 1
    ramp = table[:, 2 * clip - 1:0:-1, :]
    tail = BIAS_LANES - n_far - ramp.shape[1]
    t = jnp.concatenate([jnp.broadcast_to(far, (depth, n_far, heads)), ramp,
                         jnp.broadcast_to(far, (depth, tail, heads))], axis=1)
    return jnp.swapaxes(t, 1, 2).astype(F32)


def kernel(x_prompt, x_sample, cache_k, cache_v, state_conv, ln_ffn1_pre, ffn1_w_gate, ffn1_w_up,
           ffn1_w_down, ln_ffn1_post, ln_mix_pre, w_in, conv_w, rel_bias, g_conv_out, g_attn_out, w_out,
           ln_mix_post, ln_ffn2_pre, ffn2_w_gate, ffn2_w_up, ffn2_w_down, ln_ffn2_post):
    batch, seq, d = x_prompt.shape
    dec_batch, dec_seq, _ = x_sample.shape
    depth = w_in.shape[0]
    n_cache, n_heads, head_dim = cache_k.shape[2:]
    d_attn = n_heads * head_dim
    d_conv = conv_w.shape[2]
    keep = min(ATT_WINDOW, seq)

    tf = 2 * MXU_DIM
    assert n_heads % 2 == 0
    tiles_p = _tile_plan(batch * seq, seq)
    tiles_s = _tile_plan(dec_batch * dec_seq, dec_seq)

    gain = lambda a: a.reshape(depth, 1, -1)
    ffn1 = (gain(ln_ffn1_pre), ffn1_w_gate.astype(BF16), ffn1_w_up.astype(BF16), ffn1_w_down.astype(BF16),
            gain(ln_ffn1_post))
    ffn2 = (gain(ln_ffn2_pre), ffn2_w_gate.astype(BF16), ffn2_w_up.astype(BF16), ffn2_w_down.astype(BF16),
            gain(ln_ffn2_post))
    w_in_b = w_in.astype(BF16)
    w_out_b = w_out.astype(BF16)
    tvec = _bias_vectors(rel_bias)
    tvec_rep = jnp.repeat(tvec, n_heads, axis=-1)
    cache_kf = cache_k.reshape(depth, dec_batch, n_cache * n_heads, head_dim)
    cache_vf = cache_v.reshape(depth, dec_batch, n_cache * n_heads, head_dim)
    zero_state = jnp.zeros((1, batch, 2, d_conv), F32)

    def layer(l, h, tile, seq_len, conv_init, init_layer, kv_keep, attend):
        h = _ffn(h, *ffn1, layer=l, tm=tile.ffn, tf=tf)
        q, k, v, mix_c, conv_new, k_keep, v_keep = _proj(
            h, gain(ln_mix_pre), w_in_b, conv_w, conv_init, gain(g_conv_out), kv_keep,
            layer=l, init_layer=init_layer, tm=tile.proj, seq_len=seq_len, d_attn=d_attn)
        mix_a = attend(q, k, v)
        h = _mix_out(mix_c, mix_a, w_out_b, gain(ln_mix_post), h, layer=l, tm=tile.mix)
        h = _ffn(h, *ffn2, layer=l, tm=tile.ffn, tf=tf)
        return h, conv_new, (k_keep, v_keep)

    hp = x_prompt.reshape(batch * seq, d)
    hs = x_sample.reshape(dec_batch * dec_seq, d)
    kv_p = kv_s = None
    conv_p, conv_s = [], []
    for l in range(depth):
        attend_prompt = functools.partial(
            _attn_prompt, tvec=tvec, g_attn=gain(g_attn_out), layer=l, n_seq=batch, seq_len=seq,
            n_heads=n_heads, head_dim=head_dim)
        attend_sample = functools.partial(
            _attn_sample, cache_k=cache_kf, cache_v=cache_vf, tvec=tvec, tvec_rep=tvec_rep,
            g_attn=gain(g_attn_out), layer=l,
            n_seq=dec_batch, n_new=dec_seq, n_heads=n_heads, head_dim=head_dim)
        hp, c_p, kv_p = layer(l, hp, tiles_p, seq, zero_state, 0, kv_p, attend_prompt)
        hs, c_s, kv_s = layer(l, hs, tiles_s, dec_seq, state_conv, l, kv_s, attend_sample)
        conv_p.append(c_p)
        conv_s.append(c_s)

    return (hp.reshape(batch, seq, d), hs.reshape(dec_batch, dec_seq, d),
            kv_p[0].reshape(depth, batch, keep, n_heads, head_dim),
            kv_p[1].reshape(depth, batch, keep, n_heads, head_dim),
            jnp.stack(conv_p),
            kv_s[0].reshape(depth, dec_batch, dec_seq, n_heads, head_dim),
            kv_s[1].reshape(depth, dec_batch, dec_seq, n_heads, head_dim),
            jnp.stack(conv_s))
```

```python
import functools
from typing import NamedTuple

import jax
import jax.numpy as jnp
from jax import lax
from jax.experimental import pallas as pl
from jax.experimental.pallas import tpu as pltpu

F32 = jnp.float32
BF16 = jnp.bfloat16

CHUNK = 64
LEFT_CHUNKS = 8
ATT_WINDOW = LEFT_CHUNKS * CHUNK
EPS = 1e-6
NEG_INF = -1e30

MXU_DIM = 256
V7X_VMEM_BYTES = 64 * 1024 * 1024
VMEM_LIMIT_BYTES = V7X_VMEM_BYTES - 4 * 1024 * 1024

PAIR = 2 * CHUNK
PAIR_KEYS = ATT_WINDOW + PAIR
BIAS_LANES = PAIR_KEYS + PAIR


def _rms(x, g):
    ms = jnp.mean(x * x, axis=-1, keepdims=True)
    return x * lax.rsqrt(ms + EPS) * g


def _params(semantics):
    return pltpu.CompilerParams(dimension_semantics=semantics, vmem_limit_bytes=VMEM_LIMIT_BYTES)


def _resident(layer, shape):
    return pl.BlockSpec((None,) + tuple(shape), lambda *_: (layer,) + (0,) * len(shape),
                        pipeline_mode=pl.Buffered(1))


def _ffn_kernel(x_ref, gpre_ref, wg_ref, wu_ref, wd_ref, gpost_ref, o_ref, u_ref, *, n_chunks, last_tf, rc):
    j = pl.program_id(1)
    tm = x_ref.shape[0]
    tf = wg_ref.shape[1]
    first, last = j == 0, j == n_chunks - 1
    middle = (j > 0) & (j < n_chunks - 1)

    def gate_up(u, n):
        g = jnp.dot(u, wg_ref[:, :n], preferred_element_type=F32)
        up = jnp.dot(u, wu_ref[:, :n], preferred_element_type=F32)
        return (g * jax.nn.sigmoid(g) * up).astype(BF16)

    def down(h, n):
        return jnp.dot(h, wd_ref[:n, :], preferred_element_type=F32)

    @pl.when(first)
    def _():
        for r in range(0, tm, rc):
            u = _rms(x_ref[r:r + rc, :], gpre_ref[...]).astype(BF16)
            u_ref[r:r + rc, :] = u
            o_ref[r:r + rc, :] = down(gate_up(u, tf), tf)

    @pl.when(middle)
    def _():
        o_ref[...] += down(gate_up(u_ref[...], tf), tf)

    @pl.when(last)
    def _():
        h = gate_up(u_ref[...], last_tf)
        for r in range(0, tm, rc):
            acc = o_ref[r:r + rc, :] + down(h[r:r + rc, :], last_tf)
            o_ref[r:r + rc, :] = x_ref[r:r + rc, :] + 0.5 * _rms(acc, gpost_ref[...])


def _ffn(x, g_pre, wg, wu, wd, g_post, *, layer, tm, tf):
    t, d = x.shape
    f = wg.shape[2]
    n_chunks = pl.cdiv(f, tf)
    last_tf = f - (n_chunks - 1) * tf
    assert t % tm == 0 and n_chunks >= 2
    rc = min(tm, MXU_DIM)
    gain = pl.BlockSpec((None, 1, d), lambda i, j: (layer, 0, 0))
    return pl.pallas_call(
        functools.partial(_ffn_kernel, n_chunks=n_chunks, last_tf=last_tf, rc=rc),
        grid=(t // tm, n_chunks),
        in_specs=[
            pl.BlockSpec((tm, d), lambda i, j: (i, 0)),
            gain,
            pl.BlockSpec((None, d, tf), lambda i, j: (layer, 0, j)),
            pl.BlockSpec((None, d, tf), lambda i, j: (layer, 0, j)),
            pl.BlockSpec((None, tf, d), lambda i, j: (layer, j, 0)),
            gain,
        ],
        out_specs=pl.BlockSpec((tm, d), lambda i, j: (i, 0)),
        out_shape=jax.ShapeDtypeStruct((t, d), F32),
        scratch_shapes=[pltpu.VMEM((tm, d), BF16)],
        compiler_params=_params(("arbitrary", "arbitrary")),
        name="ffn",
    )(x, g_pre, wg, wu, wd, g_post)


def _proj_kernel(h_ref, gpre_ref, win_ref, convw_ref, cinit_ref, gc_ref, *rest,
                 tm, seq_len, d_conv, d_attn, tail_tiles, rc):
    q_ref, k_ref, v_ref, mixc_ref, cst_ref, kt_ref, vt_ref, carry_ref = rest[-8:]
    bg_parts, gx_parts, k_parts, v_parts = [], [], [], []
    for r in range(0, tm, rc):
        u = _rms(h_ref[r:r + rc, :], gpre_ref[...]).astype(BF16)

        def proj(c0, n):
            return jnp.dot(u, win_ref[:, c0:c0 + n], preferred_element_type=F32)

        bg_parts.append(proj(0, d_conv))
        gx_parts.append(proj(d_conv, d_conv) * proj(2 * d_conv, d_conv))
        q_ref[r:r + rc, :] = proj(3 * d_conv, d_attn).astype(BF16)
        k_parts.append(proj(3 * d_conv + d_attn, d_attn))
        v_parts.append(proj(3 * d_conv + 2 * d_attn, d_attn))
    bg = jnp.concatenate(bg_parts, axis=0)
    gx = jnp.concatenate(gx_parts, axis=0)
    k = jnp.concatenate(k_parts, axis=0)
    v = jnp.concatenate(v_parts, axis=0)

    g1 = pltpu.roll(gx, 1, axis=0)
    g2 = pltpu.roll(gx, 2, axis=0)
    row = lax.broadcasted_iota(jnp.int32, (tm, 1), 0)
    if seq_len >= tm:
        ti = pl.program_id(0) % (seq_len // tm)
        first = ti == 0
        halo0 = jnp.where(first, cinit_ref[0, 0:1, :], carry_ref[0:1, :])
        halo1 = jnp.where(first, cinit_ref[0, 1:2, :], carry_ref[1:2, :])
        pos = row
        carry_ref[...] = gx[tm - 2:tm, :]
        cst_ref[0] = gx[tm - 2:tm, :]
    else:
        n_seq = tm // seq_len
        halo0 = jnp.broadcast_to(cinit_ref[:, 0:1, :], (n_seq, seq_len, d_conv)).reshape(tm, d_conv)
        halo1 = jnp.broadcast_to(cinit_ref[:, 1:2, :], (n_seq, seq_len, d_conv)).reshape(tm, d_conv)
        pos = row % seq_len
        cst_ref[...] = gx.reshape(n_seq, seq_len, d_conv)[:, seq_len - 2:seq_len, :]
    g1 = jnp.where(pos == 0, halo1, g1)
    g2 = jnp.where(pos == 0, halo0, jnp.where(pos == 1, halo1, g2))
    w = convw_ref[...]
    y = bg * (g2 * w[0:1, :] + g1 * w[1:2, :] + gx * w[2:3, :])
    mixc_ref[...] = _rms(y, gc_ref[...]).astype(BF16)

    k_ref[...] = k.astype(BF16)
    v_ref[...] = v.astype(BF16)

    if seq_len >= tm:
        @pl.when(ti >= seq_len // tm - tail_tiles)
        def _():
            kt_ref[...] = k
            vt_ref[...] = v
    else:
        kt_ref[...] = k
        vt_ref[...] = v


def _proj(h, g_pre, w_in, conv_w, conv_init, g_conv, kv_keep, *, layer, init_layer, tm, seq_len, d_attn):
    t, d = h.shape
    depth, _, d_conv = conv_w.shape
    n_seq = t // seq_len
    keep = min(ATT_WINDOW, seq_len)
    if seq_len >= tm:
        assert seq_len % tm == 0 and keep % tm == 0
        tiles_per_seq = seq_len // tm
        tail_tiles = keep // tm
        seq_of = lambda i: i // tiles_per_seq
        state_block = (1, 2, d_conv)
        state_map = lambda i: (seq_of(i), 0, 0)
        tail_of = lambda i: (seq_of(i) * tail_tiles
                             + jnp.maximum(i % tiles_per_seq - (tiles_per_seq - tail_tiles), 0))
    else:
        assert tm % seq_len == 0 and keep == seq_len
        tail_tiles = 1
        state_block = (tm // seq_len, 2, d_conv)
        state_map = lambda i: (i, 0, 0)
        tail_of = lambda i: i
    tail_blocks = n_seq * keep // tm
    tail_spec = pl.BlockSpec((tm, d_attn), lambda i: (layer * tail_blocks + tail_of(i), 0))
    tile = lambda n: pl.BlockSpec((tm, n), lambda i: (i, 0))
    kern = functools.partial(_proj_kernel, tm=tm, seq_len=seq_len, d_conv=d_conv, d_attn=d_attn,
                             tail_tiles=tail_tiles, rc=min(tm, MXU_DIM))
    in_specs = [
        tile(d),
        _resident(layer, (1, d)),
        _resident(layer, w_in.shape[1:]),
        _resident(layer, conv_w.shape[1:]),
        pl.BlockSpec((None,) + state_block, lambda i: (init_layer,) + state_map(i)),
        _resident(layer, (1, d_conv)),
    ]
    args = [h, g_pre, w_in, conv_w, conv_init, g_conv]
    aliases = {len(args): 5, len(args) + 1: 6}
    in_specs += [pl.BlockSpec(memory_space=pl.ANY)] * 2
    args += list(kv_keep)
    cache_shape = jax.ShapeDtypeStruct((depth * n_seq * keep, d_attn), F32)
    assert all(a.shape == cache_shape.shape and a.dtype == F32 for a in kv_keep)
    return pl.pallas_call(
        kern,
        grid=(t // tm,),
        in_specs=in_specs,
        out_specs=[
            tile(d_attn), tile(d_attn), tile(d_attn), tile(d_conv),
            pl.BlockSpec(state_block, state_map),
            tail_spec, tail_spec,
        ],
        out_shape=[
            jax.ShapeDtypeStruct((t, d_attn), BF16),
            jax.ShapeDtypeStruct((t, d_attn), BF16),
            jax.ShapeDtypeStruct((t, d_attn), BF16),
            jax.ShapeDtypeStruct((t, d_conv), BF16),
            jax.ShapeDtypeStruct((n_seq, 2, d_conv), F32),
            cache_shape, cache_shape,
        ],
        input_output_aliases=aliases,
        scratch_shapes=[pltpu.VMEM((2, d_conv), F32)],
        compiler_params=_params(("arbitrary",)),
        name="proj",
    )(*args)


def _toeplitz_bias(tvec_ref, h, rows):
    t = jnp.broadcast_to(tvec_ref[h:h + 1, :], (rows, BIAS_LANES))
    return pltpu.roll(t, 0, axis=1, stride=1, stride_axis=0)


def _attn_prompt_kernel(q_ref, kp_ref, kc_ref, vp_ref, vc_ref, tvec_ref, ga_ref, o_ref,
                        bias_ref, y_ref, *, n_heads, head_dim, blk):
    qb = pl.program_id(1)

    @pl.when((pl.program_id(0) == 0) & (qb == 0))
    def _():
        qi = lax.broadcasted_iota(jnp.int32, (PAIR, PAIR_KEYS), 0)
        kk = lax.broadcasted_iota(jnp.int32, (PAIR, PAIR_KEYS), 1)
        lo = (qi // CHUNK) * CHUNK
        band = (kk >= lo) & (kk < lo + ATT_WINDOW + CHUNK)
        for h in range(n_heads):
            bias_ref[h] = jnp.where(band, _toeplitz_bias(tvec_ref, h, PAIR)[:, :PAIR_KEYS], NEG_INF)

    scale = head_dim ** -0.5
    upper = lax.broadcasted_iota(jnp.int32, (PAIR, 2 * head_dim), 1) < head_dim

    def attend(first_block):
        for r0 in range(0, blk, PAIR):
            k0 = blk - r0 if first_block else 0
            for h in range(0, n_heads, 2):
                c0 = h * head_dim
                cols = slice(c0, c0 + 2 * head_dim)
                q2 = q_ref[r0:r0 + PAIR, cols]
                if first_block:
                    k2 = kc_ref[0:r0 + PAIR, cols]
                    v2 = vc_ref[0:r0 + PAIR, cols]
                else:
                    k2 = jnp.concatenate([kp_ref[r0:blk, cols], kc_ref[0:r0 + PAIR, cols]], axis=0)
                    v2 = jnp.concatenate([vp_ref[r0:blk, cols], vc_ref[0:r0 + PAIR, cols]], axis=0)
                zero = jnp.zeros_like(q2)
                qbd = jnp.concatenate([jnp.where(upper, q2, zero), jnp.where(upper, zero, q2)], axis=0)
                s = lax.dot_general(qbd, k2, (((1,), (1,)), ((), ())), preferred_element_type=F32)
                bias = bias_ref[h:h + 2, :, k0:].reshape(2 * PAIR, PAIR_KEYS - k0)
                s = s * scale + bias
                e = jnp.exp(s - jnp.max(s, axis=-1, keepdims=True))
                l = jnp.sum(e, axis=-1, keepdims=True)
                o = jnp.dot(e.astype(BF16), v2, preferred_element_type=F32)
                y_ref[r0:r0 + PAIR, c0:c0 + head_dim] = o[:PAIR, :head_dim] / l[:PAIR]
                y_ref[r0:r0 + PAIR, c0 + head_dim:c0 + 2 * head_dim] = o[PAIR:, head_dim:] / l[PAIR:]
        o_ref[...] = _rms(y_ref[...], ga_ref[...]).astype(BF16)

    pl.when(qb == 0)(functools.partial(attend, True))
    pl.when(qb != 0)(functools.partial(attend, False))


def _attn_prompt(q, k, v, tvec, g_attn, *, layer, n_seq, seq_len, n_heads, head_dim):
    t, d_attn = q.shape
    blk = ATT_WINDOW
    assert seq_len % blk == 0 and blk % PAIR == 0
    nb = seq_len // blk
    cur = pl.BlockSpec((blk, d_attn), lambda b, i: (b * nb + i, 0))
    prev = pl.BlockSpec((blk, d_attn), lambda b, i: (b * nb + jnp.maximum(i - 1, 0), 0))
    kern = functools.partial(_attn_prompt_kernel, n_heads=n_heads, head_dim=head_dim, blk=blk)
    return pl.pallas_call(
        kern,
        grid=(n_seq, nb),
        in_specs=[cur, prev, cur, prev, cur,
                  _resident(layer, tvec.shape[1:]), _resident(layer, (1, d_attn))],
        out_specs=cur,
        out_shape=jax.ShapeDtypeStruct((t, d_attn), BF16),
        scratch_shapes=[
            pltpu.VMEM((n_heads, PAIR, PAIR_KEYS), F32),
            pltpu.VMEM((blk, d_attn), F32),
        ],
        compiler_params=_params(("arbitrary", "arbitrary")),
        name="attn_prompt",
    )(q, k, k, v, v, tvec, g_attn)


def _attn_sample_kernel(q_ref, kn_ref, vn_ref, kc_ref, vc_ref, tvec_ref, tvec_rep_ref, ga_ref, o_ref,
                        bias_ref, y_ref, *, n_heads, head_dim, n_new, n_cache):
    cols = n_cache * n_heads

    @pl.when(pl.program_id(0) == 0)
    def _():
        col_head = lax.broadcasted_iota(jnp.int32, (n_new, cols), 1) % n_heads
        for h in range(n_heads):
            t = jnp.broadcast_to(tvec_rep_ref[h:h + 1, :], (n_new, tvec_rep_ref.shape[1]))
            shifted = pltpu.roll(t, 0, axis=1, stride=n_heads, stride_axis=0)[:, :cols]
            bias_ref[h * n_new:(h + 1) * n_new, :] = jnp.where(col_head == h, shifted, NEG_INF)

    scale = head_dim ** -0.5
    nt = (((1,), (1,)), ((), ()))
    head = lambda ref, h: ref[:, h * head_dim:(h + 1) * head_dim]
    q_all = jnp.concatenate([head(q_ref, h) for h in range(n_heads)], axis=0)
    sc = (lax.dot_general(q_all, kc_ref[...].astype(BF16), nt, preferred_element_type=F32) * scale
          + bias_ref[...])
    sn = jnp.concatenate(
        [lax.dot_general(head(q_ref, h), head(kn_ref, h), nt, preferred_element_type=F32) * scale
         + _toeplitz_bias(tvec_ref, h, n_new)[:, n_cache:n_cache + n_new] for h in range(n_heads)], axis=0)
    m = jnp.maximum(jnp.max(sc, axis=-1, keepdims=True), jnp.max(sn, axis=-1, keepdims=True))
    ec = jnp.exp(sc - m)
    en = jnp.exp(sn - m)
    l = jnp.sum(ec, axis=-1, keepdims=True) + jnp.sum(en, axis=-1, keepdims=True)
    o = jnp.dot(ec.astype(BF16), vc_ref[...].astype(BF16), preferred_element_type=F32)
    o = o + jnp.concatenate(
        [jnp.dot(en[h * n_new:(h + 1) * n_new, :].astype(BF16), head(vn_ref, h), preferred_element_type=F32)
         for h in range(n_heads)], axis=0)
    o = o / l
    for h in range(n_heads):
        y_ref[:, h * head_dim:(h + 1) * head_dim] = o[h * n_new:(h + 1) * n_new, :]
    o_ref[...] = _rms(y_ref[...], ga_ref[...]).astype(BF16)


def _attn_sample(q, k, v, cache_k, cache_v, tvec, tvec_rep, g_attn, *, layer, n_seq, n_new, n_heads, head_dim):
    t, d_attn = q.shape
    n_cache = cache_k.shape[2] // n_heads
    assert n_cache == ATT_WINDOW and n_new <= PAIR
    new = pl.BlockSpec((n_new, d_attn), lambda b: (b, 0))
    cache = pl.BlockSpec((None, None, n_cache * n_heads, head_dim), lambda b: (layer, b, 0, 0))
    kern = functools.partial(_attn_sample_kernel, n_heads=n_heads, head_dim=head_dim,
                             n_new=n_new, n_cache=n_cache)
    return pl.pallas_call(
        kern,
        grid=(n_seq,),
        in_specs=[new, new, new, cache, cache, _resident(layer, tvec.shape[1:]),
                  _resident(layer, tvec_rep.shape[1:]), _resident(layer, (1, d_attn))],
        out_specs=new,
        out_shape=jax.ShapeDtypeStruct((t, d_attn), BF16),
        scratch_shapes=[pltpu.VMEM((n_heads * n_new, n_cache * n_heads), F32),
                        pltpu.VMEM((n_new, d_attn), F32)],
        compiler_params=_params(("arbitrary",)),
        name="attn_sample",
    )(q, k, v, cache_k, cache_v, tvec, tvec_rep, g_attn)


def _mix_out_kernel(mc_ref, ma_ref, wo_ref, gpost_ref, h_ref, o_ref, *, d_conv, rc):
    for r in range(0, h_ref.shape[0], rc):
        z = (jnp.dot(mc_ref[r:r + rc, :], wo_ref[0:d_conv, :], preferred_element_type=F32)
             + jnp.dot(ma_ref[r:r + rc, :], wo_ref[d_conv:, :], preferred_element_type=F32))
        o_ref[r:r + rc, :] = h_ref[r:r + rc, :] + _rms(z, gpost_ref[...])


def _mix_out(mix_c, mix_a, w_out, g_post, h, *, layer, tm):
    t, d = h.shape
    d_conv = mix_c.shape[1]
    d_attn = mix_a.shape[1]
    tile = lambda n: pl.BlockSpec((tm, n), lambda i: (i, 0))
    return pl.pallas_call(
        functools.partial(_mix_out_kernel, d_conv=d_conv, rc=min(tm, MXU_DIM)),
        grid=(t // tm,),
        in_specs=[tile(d_conv), tile(d_attn), _resident(layer, w_out.shape[1:]),
                  _resident(layer, (1, d)), tile(d)],
        out_specs=tile(d),
        out_shape=jax.ShapeDtypeStruct((t, d), F32),
        compiler_params=_params(("arbitrary",)),
        name="mix_out",
    )(mix_c, mix_a, w_out, g_post, h)


class _Tiles(NamedTuple):
    ffn: int
    proj: int
    mix: int


def _tile_plan(n_tokens, seq_len):
    fit = lambda rows: rows if n_tokens % rows == 0 and n_tokens >= rows else n_tokens
    proj = fit(2 * MXU_DIM)
    if seq_len >= proj:
        assert seq_len % proj == 0 and min(ATT_WINDOW, seq_len) % proj == 0
    return _Tiles(ffn=fit(4 * MXU_DIM), proj=proj, mix=fit(4 * MXU_DIM))


def _bias_vectors(table):
    depth, n_rel, heads = table.shape
    clip = (n_rel - 1) // 2
    far = table[:, 2 * clip:2 * clip + 1, :]
    n_far = ATT_WINDOW - clip + 1
    ramp = table[:, 2 * clip - 1:0:-1, :]
    tail = BIAS_LANES - n_far - ramp.shape[1]
    t = jnp.concatenate([jnp.broadcast_to(far, (depth, n_far, heads)), ramp,
                         jnp.broadcast_to(far, (depth, tail, heads))], axis=1)
    return jnp.swapaxes(t, 1, 2).astype(F32)


def kernel(x_prompt, x_sample, cache_k, cache_v, state_conv, ln_ffn1_pre, ffn1_w_gate, ffn1_w_up,
           ffn1_w_down, ln_ffn1_post, ln_mix_pre, w_in, conv_w, rel_bias, g_conv_out, g_attn_out, w_out,
           ln_mix_post, ln_ffn2_pre, ffn2_w_gate, ffn2_w_up, ffn2_w_down, ln_ffn2_post):
    batch, seq, d = x_prompt.shape
    dec_batch, dec_seq, _ = x_sample.shape
    depth = w_in.shape[0]
    n_cache, n_heads, head_dim = cache_k.shape[2:]
    d_attn = n_heads * head_dim
    d_conv = conv_w.shape[2]
    keep = min(ATT_WINDOW, seq)

    tf = 2 * MXU_DIM
    assert n_heads % 2 == 0
    tiles_p = _tile_plan(batch * seq, seq)
    tiles_s = _tile_plan(dec_batch * dec_seq, dec_seq)

    gain = lambda a: a.reshape(depth, 1, -1)
    ffn1 = (gain(ln_ffn1_pre), ffn1_w_gate.astype(BF16), ffn1_w_up.astype(BF16), ffn1_w_down.astype(BF16),
            gain(ln_ffn1_post))
    ffn2 = (gain(ln_ffn2_pre), ffn2_w_gate.astype(BF16), ffn2_w_up.astype(BF16), ffn2_w_down.astype(BF16),
            gain(ln_ffn2_post))
    w_in_b = w_in.astype(BF16)
    w_out_b = w_out.astype(BF16)
    tvec = _bias_vectors(rel_bias)
    tvec_rep = jnp.repeat(tvec, n_heads, axis=-1)
    cache_kf = cache_k.reshape(depth, dec_batch, n_cache * n_heads, head_dim)
    cache_vf = cache_v.reshape(depth, dec_batch, n_cache * n_heads, head_dim)
    zero_state = jnp.zeros((1, batch, 2, d_conv), F32)

    def layer(l, h, tile, seq_len, conv_init, init_layer, kv_keep, attend):
        h = _ffn(h, *ffn1, layer=l, tm=tile.ffn, tf=tf)
        q, k, v, mix_c, conv_new, k_keep, v_keep = _proj(
            h, gain(ln_mix_pre), w_in_b, conv_w, conv_init, gain(g_conv_out), kv_keep,
            layer=l, init_layer=init_layer, tm=tile.proj, seq_len=seq_len, d_attn=d_attn)
        mix_a = attend(q, k, v)
        h = _mix_out(mix_c, mix_a, w_out_b, gain(ln_mix_post), h, layer=l, tm=tile.mix)
        h = _ffn(h, *ffn2, layer=l, tm=tile.ffn, tf=tf)
        return h, conv_new, (k_keep, v_keep)

    hp = x_prompt.reshape(batch * seq, d)
    hs = x_sample.reshape(dec_batch * dec_seq, d)
    kv_p = tuple(jnp.zeros((depth * batch * keep, d_attn), F32) for _ in range(2))
    kv_s = tuple(jnp.zeros((depth * dec_batch * dec_seq, d_attn), F32) for _ in range(2))
    conv_p, conv_s = [], []
    for l in range(depth):
        attend_prompt = functools.partial(
            _attn_prompt, tvec=tvec, g_attn=gain(g_attn_out), layer=l, n_seq=batch, seq_len=seq,
            n_heads=n_heads, head_dim=head_dim)
        attend_sample = functools.partial(
            _attn_sample, cache_k=cache_kf, cache_v=cache_vf, tvec=tvec, tvec_rep=tvec_rep,
            g_attn=gain(g_attn_out), layer=l,
            n_seq=dec_batch, n_new=dec_seq, n_heads=n_heads, head_dim=head_dim)
        hp, c_p, kv_p = layer(l, hp, tiles_p, seq, zero_state, 0, kv_p, attend_prompt)
        hs, c_s, kv_s = layer(l, hs, tiles_s, dec_seq, state_conv, l, kv_s, attend_sample)
        conv_p.append(c_p)
        conv_s.append(c_s)

    return (hp.reshape(batch, seq, d), hs.reshape(dec_batch, dec_seq, d),
            kv_p[0].reshape(depth, batch, keep, n_heads, head_dim),
            kv_p[1].reshape(depth, batch, keep, n_heads, head_dim),
            jnp.stack(conv_p),
            kv_s[0].reshape(depth, dec_batch, dec_seq, n_heads, head_dim),
            kv_s[1].reshape(depth, dec_batch, dec_seq, n_heads, head_dim),
            jnp.stack(conv_s))
```
